```python
import jax, jax.numpy as jnp
from jax import lax
import numpy as np

D_MODEL = 1024
BATCH = 2
SEQ = 16384
DEPTH = 2
DEC_BATCH = 4
DEC_SEQ = 8192
PAST_LEN = 128

RET_HEADS = 4
RET_HEAD_DIM = 128
RET_WIDTH = RET_HEADS * RET_HEAD_DIM
CHUNK = 128
FFT_GROUPS = 4
FFT_GROUP_DIM = 128
FFT_WIDTH = FFT_GROUPS * FFT_GROUP_DIM
D_FF = 2816
RMS_EPS = 1e-6
GN_EPS = 1e-6
ROPE_BASE = 10000.0
SPLITS = [RET_WIDTH, 2 * RET_WIDTH, 3 * RET_WIDTH, 4 * RET_WIDTH,
          4 * RET_WIDTH + FFT_WIDTH, 4 * RET_WIDTH + FFT_WIDTH + D_MODEL]
IN_WIDTH = 4 * RET_WIDTH + FFT_WIDTH + 2 * D_MODEL

kernel_name = "hybrid_retention_fnet_macaron_encoder"


def _rmsnorm(x, g):
    xf = x.astype(jnp.float32)
    y = xf * lax.rsqrt(jnp.mean(xf * xf, axis=-1, keepdims=True) + RMS_EPS)
    return (y * g.astype(jnp.float32)).astype(x.dtype)


def _swiglu(h, w_in, w_out):
    a, b = jnp.split(h @ w_in, 2, axis=-1)
    return (jax.nn.silu(a) * b) @ w_out


def _rotary(x, pos):
    half = x.shape[-1] // 2
    inv_freq = 1.0 / (ROPE_BASE ** jnp.linspace(0.0, 1.0, half, dtype=jnp.float32))
    ang = pos[:, None] * inv_freq[None, :]
    cos = jnp.cos(ang)[None, :, None, :]
    sin = jnp.sin(ang)[None, :, None, :]
    xf = x.astype(jnp.float32)
    x1, x2 = xf[..., 0::2], xf[..., 1::2]
    return jnp.stack([x1 * cos - x2 * sin, x1 * sin + x2 * cos], axis=-1).reshape(x.shape)


def _to_chunks(x):
    b, s, h, d = x.shape
    return x.reshape(b, s // CHUNK, CHUNK, h, d).transpose(0, 3, 1, 2, 4)


def _from_chunks(x):
    b, h, n, c, d = x.shape
    return x.transpose(0, 2, 3, 1, 4).reshape(b, n * c, h, d)


def _retention_one_dir(q, k, v, log_decay, include_diag):
    qc, kc, vc = _to_chunks(q), _to_chunks(k), _to_chunks(v)
    ld = log_decay.astype(jnp.float32)
    idx = jnp.arange(CHUNK)
    diff = (idx[:, None] - idx[None, :]).astype(jnp.float32)
    mask = (idx[:, None] >= idx[None, :]) if include_diag else (idx[:, None] > idx[None, :])
    dmat = jnp.where(mask, jnp.exp(ld[:, None, None] * jnp.where(mask, diff, 0.0)), 0.0)
    scores = jnp.einsum('bhnid,bhnjd->bhnij', qc, kc) * dmat[None, :, None]
    intra = jnp.einsum('bhnij,bhnje->bhnie', scores, vc)
    k_decay = jnp.exp(ld[:, None] * (CHUNK - 1 - idx).astype(jnp.float32))
    kv = jnp.einsum('bhnjd,bhnje->bhnde', kc * k_decay[None, :, None, :, None], vc)
    chunk_decay = jnp.exp(ld * CHUNK)[None, :, None, None]

    def step(state, kv_n):
        return chunk_decay * state + kv_n, state

    b, h = kv.shape[0], kv.shape[1]
    init = jnp.zeros((b, h, kv.shape[3], kv.shape[4]), kv.dtype)
    _, s_prev = lax.scan(step, init, jnp.moveaxis(kv, 2, 0))
    s_prev = jnp.moveaxis(s_prev, 0, 2)
    q_decay = jnp.exp(ld[:, None] * (idx + 1).astype(jnp.float32))
    inter = jnp.einsum('bhnid,bhnde->bhnie', qc * q_decay[None, :, None, :, None], s_prev)
    return _from_chunks(intra + inter)


def _bidir_retention(q, k, v, ld_fwd, ld_bwd):
    flip = lambda t: t[:, ::-1]
    fwd = _retention_one_dir(q, k, v, ld_fwd, True)
    bwd = flip(_retention_one_dir(flip(q), flip(k), flip(v), ld_bwd, False))
    return fwd + bwd


def _mixer(h, w_in, ld_fwd, ld_bwd, w_ret_out, w_fft_out, w_mix_out):
    b, s, _ = h.shape
    proj = h @ w_in
    q, k, v, g, f, gate_ret, gate_fft = jnp.split(proj, SPLITS, axis=-1)
    pos = jnp.arange(s, dtype=jnp.float32)
    q = _rotary(q.reshape(b, s, RET_HEADS, RET_HEAD_DIM), pos)
    k = _rotary(k.reshape(b, s, RET_HEADS, RET_HEAD_DIM), pos) * (RET_HEAD_DIM ** -0.5)
    v = v.reshape(b, s, RET_HEADS, RET_HEAD_DIM).astype(jnp.float32)
    y = _bidir_retention(q, k, v, ld_fwd, ld_bwd)
    mu = jnp.mean(y, axis=-1, keepdims=True)
    yc = y - mu
    yn = yc * lax.rsqrt(jnp.mean(yc * yc, axis=-1, keepdims=True) + GN_EPS)
    ret = (jax.nn.silu(g.astype(jnp.float32)) * yn.reshape(b, s, RET_WIDTH)).astype(h.dtype) @ w_ret_out
    fr = jnp.fft.fftn(f.reshape(b, s, FFT_GROUPS, FFT_GROUP_DIM).astype(jnp.float32),
                      axes=(1, 3), norm="ortho").real
    fou = fr.reshape(b, s, FFT_WIDTH).astype(h.dtype) @ w_fft_out
    merged = jax.nn.sigmoid(gate_ret) * ret + jax.nn.sigmoid(gate_fft) * fou
    return merged @ w_mix_out


def _trunk(x, g_ffn1, w_ffn1_in, w_ffn1_out, g_mix, w_in, ret_log_decay_fwd, ret_log_decay_bwd,
           w_ret_out, w_fft_out, w_mix_out, g_ffn2, w_ffn2_in, w_ffn2_out, g_final):
    for l in range(DEPTH):
        x = x + 0.5 * _swiglu(_rmsnorm(x, g_ffn1[l]), w_ffn1_in[l], w_ffn1_out[l])
        x = x + _mixer(_rmsnorm(x, g_mix[l]), w_in[l], ret_log_decay_fwd[l], ret_log_decay_bwd[l],
                       w_ret_out[l], w_fft_out[l], w_mix_out[l])
        x = x + 0.5 * _swiglu(_rmsnorm(x, g_ffn2[l]), w_ffn2_in[l], w_ffn2_out[l])
    return _rmsnorm(x, g_final)


def setup_inputs(seed: int = 0) -> dict:
    key = jax.random.key(seed)
    ks = jax.random.split(key, 18)
    f32 = jnp.float32
    nrm = lambda k, shape, scale: jax.random.normal(k, shape, f32) * scale
    base_ld = jnp.log(1.0 - 2.0 ** (-5.0 - jnp.arange(RET_HEADS, dtype=f32)))
    return {
        "x_prompt": nrm(ks[0], (BATCH, SEQ, D_MODEL), 1.0),
        "x_sample": nrm(ks[1], (DEC_BATCH, DEC_SEQ, D_MODEL), 1.0),
        "g_ffn1": 1.0 + nrm(ks[2], (DEPTH, D_MODEL), 0.01),
        "w_ffn1_in": nrm(ks[3], (DEPTH, D_MODEL, 2 * D_FF), D_MODEL ** -0.5),
        "w_ffn1_out": nrm(ks[4], (DEPTH, D_FF, D_MODEL), D_FF ** -0.5),
        "g_mix": 1.0 + nrm(ks[5], (DEPTH, D_MODEL), 0.01),
        "w_in": nrm(ks[6], (DEPTH, D_MODEL, IN_WIDTH), D_MODEL ** -0.5),
        "ret_log_decay_fwd": base_ld[None, :] * (1.0 + nrm(ks[7], (DEPTH, RET_HEADS), 0.05)),
        "ret_log_decay_bwd": base_ld[None, :] * (1.0 + nrm(ks[8], (DEPTH, RET_HEADS), 0.05)),
        "w_ret_out": nrm(ks[9], (DEPTH, RET_WIDTH, D_MODEL), RET_WIDTH ** -0.5),
        "w_fft_out": nrm(ks[10], (DEPTH, FFT_WIDTH, D_MODEL), FFT_WIDTH ** -0.5),
        "w_mix_out": nrm(ks[11], (DEPTH, D_MODEL, D_MODEL), D_MODEL ** -0.5),
        "g_ffn2": 1.0 + nrm(ks[12], (DEPTH, D_MODEL), 0.01),
        "w_ffn2_in": nrm(ks[13], (DEPTH, D_MODEL, 2 * D_FF), D_MODEL ** -0.5),
        "w_ffn2_out": nrm(ks[14], (DEPTH, D_FF, D_MODEL), D_FF ** -0.5),
        "g_final": 1.0 + nrm(ks[15], (D_MODEL,), 0.01),
    }


def reference(x_prompt, x_sample, g_ffn1, w_ffn1_in, w_ffn1_out, g_mix, w_in, ret_log_decay_fwd,
              ret_log_decay_bwd, w_ret_out, w_fft_out, w_mix_out, g_ffn2, w_ffn2_in, w_ffn2_out, g_final):
    y_prompt = _trunk(x_prompt, g_ffn1, w_ffn1_in, w_ffn1_out, g_mix, w_in, ret_log_decay_fwd,
                      ret_log_decay_bwd, w_ret_out, w_fft_out, w_mix_out, g_ffn2, w_ffn2_in, w_ffn2_out, g_final)
    y_sample = _trunk(x_sample, g_ffn1, w_ffn1_in, w_ffn1_out, g_mix, w_in, ret_log_decay_fwd,
                      ret_log_decay_bwd, w_ret_out, w_fft_out, w_mix_out, g_ffn2, w_ffn2_in, w_ffn2_out, g_final)
    return (y_prompt, y_sample)
```

```python
import functools

import jax
import jax.numpy as jnp
from jax import lax
from jax.experimental import pallas as pl
from jax.experimental.pallas import tpu as pltpu

F32 = jnp.float32
BF16 = jnp.bfloat16

RET_HEADS = 4
HEAD_DIM = 128
RET_WIDTH = RET_HEADS * HEAD_DIM
FFT_GROUPS = 4
FFT_GROUP_DIM = 128
FFT_WIDTH = FFT_GROUPS * FFT_GROUP_DIM
RMS_EPS = 1e-6
GN_EPS = 1e-6
ROPE_BASE = 10000.0

V7X_LANES = 128
V7X_SUBLANES_F32 = 8
V7X_MXU_COLUMNS = 256
V7X_VMEM_BYTES = 64 * 1024 * 1024
VMEM_LIMIT_BYTES = V7X_VMEM_BYTES - 8 * 1024 * 1024

TOKEN_TILE = 1024
RET_CHUNK = 512
FF_CHUNK = V7X_MXU_COLUMNS
DFT_N2 = 128
DFT_ROWS = V7X_SUBLANES_F32


def _params(n_grid_dims):
    return pltpu.CompilerParams(
        dimension_semantics=("parallel",) * n_grid_dims,
        vmem_limit_bytes=VMEM_LIMIT_BYTES,
    )


def _resident(shape):
    return pl.BlockSpec(shape, lambda *_: (0,) * len(shape), pipeline_mode=pl.Buffered(1))


def _smem():
    return pl.BlockSpec(memory_space=pltpu.SMEM)


def _rmsnorm(x, g):
    return x * lax.rsqrt(jnp.mean(x * x, axis=-1, keepdims=True) + RMS_EPS) * g


def _dot(a, b):
    return jnp.dot(a, b, preferred_element_type=F32)


def _ffn_kernel(x_ref, g_ref, win_ref, wout_ref, gfin_ref, o_ref, h_ref, *, d_ff, final_norm):
    x = x_ref[...]
    xn = _rmsnorm(x, g_ref[...]).astype(BF16)
    for lo in range(0, d_ff, FF_CHUNK):
        a = _dot(xn, win_ref[:, lo:lo + FF_CHUNK])
        b = _dot(xn, win_ref[:, d_ff + lo:d_ff + lo + FF_CHUNK])
        h_ref[:, lo:lo + FF_CHUNK] = (jax.nn.silu(a) * b).astype(BF16)
    out = x + 0.5 * _dot(h_ref[...], wout_ref[...])
    if final_norm:
        out = _rmsnorm(out, gfin_ref[...])
    o_ref[...] = out


def _ffn(x, g, w_in, w_out, g_final, *, final_norm):
    n_tok, d = x.shape
    d_ff = w_out.shape[0]
    tm = TOKEN_TILE
    assert n_tok % tm == 0 and d_ff % FF_CHUNK == 0
    tile = pl.BlockSpec((tm, d), lambda i: (i, 0))
    return pl.pallas_call(
        functools.partial(_ffn_kernel, d_ff=d_ff, final_norm=final_norm),
        out_shape=jax.ShapeDtypeStruct((n_tok, d), F32),
        grid=(n_tok // tm,),
        in_specs=[tile, _resident((1, d)), _resident((d, 2 * d_ff)), _resident((d_ff, d)), _resident((1, d))],
        out_specs=tile,
        scratch_shapes=[pltpu.VMEM((tm, d_ff), BF16)],
        compiler_params=_params(1),
        name="ffn",
    )(x, g, w_in, w_out, g_final)


def _mixer_in_kernel(ldf_ref, ldb_ref, x_ref, g_ref, w_ref, cos_ref, sin_ref,
                     q_ref, k_ref, v_ref, gs_ref, f_ref, kv_ref, *, tm):
    xn = _rmsnorm(x_ref[...], g_ref[...]).astype(BF16)
    cos = cos_ref[...]
    sin = sin_ref[...]
    q = _dot(xn, w_ref[:, 0:RET_WIDTH])
    k = _dot(xn, w_ref[:, RET_WIDTH:2 * RET_WIDTH])
    v = _dot(xn, w_ref[:, 2 * RET_WIDTH:3 * RET_WIDTH]).astype(BF16)
    v_ref[...] = v
    gs_ref[...] = _dot(xn, w_ref[:, 3 * RET_WIDTH:4 * RET_WIDTH])
    f_ref[...] = _dot(xn, w_ref[:, 4 * RET_WIDTH:4 * RET_WIDTH + FFT_WIDTH])
    row = lax.broadcasted_iota(jnp.int32, (tm, HEAD_DIM), 0).astype(F32)
    k_scale = HEAD_DIM ** -0.5
    for h in range(RET_HEADS):
        cols = slice(h * HEAD_DIM, (h + 1) * HEAD_DIM)
        qh = q[:, cols]
        q_ref[:, cols] = (qh * cos + pltpu.roll(qh, HEAD_DIM // 2, axis=1) * sin).astype(BF16)
        kh = k[:, cols]
        kr = (kh * cos + pltpu.roll(kh, HEAD_DIM // 2, axis=1) * sin) * k_scale
        k_ref[:, cols] = kr.astype(BF16)
        vh = v[:, cols]
        kf = (kr * jnp.exp(ldf_ref[h] * (tm - 1.0 - row))).astype(BF16)
        kb = (kr * jnp.exp(ldb_ref[h] * row)).astype(BF16)
        tn = (((0,), (0,)), ((), ()))
        kv_ref[2 * h] = lax.dot_general(kf, vh, tn, preferred_element_type=F32)
        kv_ref[2 * h + 1] = lax.dot_general(kb, vh, tn, preferred_element_type=F32)


def _mixer_in(x, g, w, cos, sin, ld_f, ld_b):
    b, s, d = x.shape
    tm = TOKEN_TILE
    nt = s // tm
    tok = lambda width: pl.BlockSpec((None, tm, width), lambda i, j: (i, j, 0))
    rot = pl.BlockSpec((tm, HEAD_DIM), lambda i, j: (j, 0))
    kv_spec = pl.BlockSpec((None, None, 2 * RET_HEADS, HEAD_DIM, HEAD_DIM), lambda i, j: (i, j, 0, 0, 0))
    act = lambda dtype: jax.ShapeDtypeStruct((b, s, RET_WIDTH), dtype)
    return pl.pallas_call(
        functools.partial(_mixer_in_kernel, tm=tm),
        out_shape=(act(BF16), act(BF16), act(BF16), act(F32), act(F32),
                   jax.ShapeDtypeStruct((b, nt, 2 * RET_HEADS, HEAD_DIM, HEAD_DIM), F32)),
        grid=(b, nt),
        in_specs=[_smem(), _smem(), tok(d), _resident((1, d)), _resident(w.shape), rot, rot],
        out_specs=(tok(RET_WIDTH), tok(RET_WIDTH), tok(RET_WIDTH), tok(RET_WIDTH), tok(FFT_WIDTH), kv_spec),
        compiler_params=_params(2),
        name="mixer_in",
    )(ld_f, ld_b, x, g, w, cos, sin)


def _ret_states_kernel(ldf_ref, ldb_ref, kv_ref, st_ref, *, nt, tm):
    for h in range(RET_HEADS):
        a_f = jnp.exp(jnp.full((HEAD_DIM, HEAD_DIM), ldf_ref[h] * tm, F32))
        a_b = jnp.exp(jnp.full((HEAD_DIM, HEAD_DIM), ldb_ref[h] * tm, F32))
        state = jnp.zeros((HEAD_DIM, HEAD_DIM), F32)
        for t in range(nt):
            st_ref[t, 2 * h] = state
            state = a_f * state + kv_ref[t, 2 * h]
        state = jnp.zeros((HEAD_DIM, HEAD_DIM), F32)
        for t in reversed(range(nt)):
            st_ref[t, 2 * h + 1] = state
            state = a_b * state + kv_ref[t, 2 * h + 1]


def _ret_states(kv, ld_f, ld_b):
    b, nt = kv.shape[:2]
    spec = pl.BlockSpec((None,) + kv.shape[1:], lambda i: (i, 0, 0, 0, 0))
    return pl.pallas_call(
        functools.partial(_ret_states_kernel, nt=nt, tm=TOKEN_TILE),
        out_shape=jax.ShapeDtypeStruct(kv.shape, F32),
        grid=(b,),
        in_specs=[_smem(), _smem(), spec],
        out_specs=spec,
        compiler_params=_params(1),
        name="ret_states",
    )(ld_f, ld_b, kv)


def _ret_decay_kernel(ldf_ref, ldb_ref, mask_ref, dec_ref, *, rc):
    i = lax.broadcasted_iota(jnp.int32, (rc, rc), 0)
    j = lax.broadcasted_iota(jnp.int32, (rc, rc), 1)
    dist = (i - j).astype(F32)
    row = lax.broadcasted_iota(jnp.int32, (rc, HEAD_DIM), 0).astype(F32)
    for h in range(RET_HEADS):
        ld_f = ldf_ref[h]
        ld_b = ldb_ref[h]
        mask_ref[h] = jnp.exp(jnp.where(i >= j, ld_f * dist, -ld_b * dist))
        dec_ref[h, 0] = jnp.exp(ld_f * (row + 1.0))
        dec_ref[h, 1] = jnp.exp(ld_b * (rc - row))
        dec_ref[h, 2] = jnp.exp(ld_f * (rc - 1.0 - row))
        dec_ref[h, 3] = jnp.exp(ld_b * row)


def _ret_decay(ld_f, ld_b, rc):
    return pl.pallas_call(
        functools.partial(_ret_decay_kernel, rc=rc),
        out_shape=(jax.ShapeDtypeStruct((RET_HEADS, rc, rc), F32),
                   jax.ShapeDtypeStruct((RET_HEADS, 4, rc, HEAD_DIM), F32)),
        in_specs=[_smem(), _smem()],
        compiler_params=pltpu.CompilerParams(vmem_limit_bytes=VMEM_LIMIT_BYTES),
        name="ret_decay",
    )(ld_f, ld_b)


def _retention_kernel(ldf_ref, ldb_ref, q_ref, k_ref, v_ref, gs_ref, st_ref, mask_ref, dec_ref, o_ref, *, tm, rc):
    nc = tm // rc
    tn = (((0,), (0,)), ((), ()))
    nt_dims = (((1,), (1,)), ((), ()))
    for h in range(RET_HEADS):
        cols = slice(h * HEAD_DIM, (h + 1) * HEAD_DIM)
        q_dec_f, q_dec_b, k_dec_f, k_dec_b = (dec_ref[h, n] for n in range(4))
        rows = [slice(c * rc, (c + 1) * rc) for c in range(nc)]
        fwd = [st_ref[2 * h]]
        bwd = [st_ref[2 * h + 1]]
        if nc > 1:
            a_f = jnp.exp(jnp.full((HEAD_DIM, HEAD_DIM), ldf_ref[h] * rc, F32))
            a_b = jnp.exp(jnp.full((HEAD_DIM, HEAD_DIM), ldb_ref[h] * rc, F32))
            for c in range(nc - 1):
                kc = k_ref[rows[c], cols].astype(F32)
                kv = lax.dot_general((kc * k_dec_f).astype(BF16), v_ref[rows[c], cols], tn, preferred_element_type=F32)
                fwd.append(a_f * fwd[-1] + kv)
            for c in range(nc - 1, 0, -1):
                kc = k_ref[rows[c], cols].astype(F32)
                kv = lax.dot_general((kc * k_dec_b).astype(BF16), v_ref[rows[c], cols], tn, preferred_element_type=F32)
                bwd.append(a_b * bwd[-1] + kv)
            bwd = bwd[::-1]
        for c in range(nc):
            qc = q_ref[rows[c], cols]
            kc = k_ref[rows[c], cols]
            vc = v_ref[rows[c], cols]
            scores = lax.dot_general(qc, kc, nt_dims, preferred_element_type=F32)
            intra = _dot((scores * mask_ref[h]).astype(BF16), vc)
            qf = qc.astype(F32)
            q_both = jnp.concatenate([(qf * q_dec_f).astype(BF16), (qf * q_dec_b).astype(BF16)], axis=1)
            s_both = jnp.concatenate([fwd[c], bwd[c]], axis=0).astype(BF16)
            y = intra + _dot(q_both, s_both)
            yc = y - jnp.mean(y, axis=-1, keepdims=True)
            yn = yc * lax.rsqrt(jnp.mean(yc * yc, axis=-1, keepdims=True) + GN_EPS)
            o_ref[rows[c], cols] = (jax.nn.silu(gs_ref[rows[c], cols]) * yn).astype(BF16)


def _retention(q, k, v, gs, st, mask, dec, ld_f, ld_b):
    b, s, _ = q.shape
    tm = TOKEN_TILE
    rc = mask.shape[-1]
    tok = pl.BlockSpec((None, tm, RET_WIDTH), lambda i, j: (i, j, 0))
    st_spec = pl.BlockSpec((None, None, 2 * RET_HEADS, HEAD_DIM, HEAD_DIM), lambda i, j: (i, j, 0, 0, 0))
    return pl.pallas_call(
        functools.partial(_retention_kernel, tm=tm, rc=rc),
        out_shape=jax.ShapeDtypeStruct((b, s, RET_WIDTH), BF16),
        grid=(b, s // tm),
        in_specs=[_smem(), _smem(), tok, tok, tok, tok, st_spec, _resident(mask.shape), _resident(dec.shape)],
        out_specs=tok,
        compiler_params=_params(2),
        name="retention",
    )(ld_f, ld_b, q, k, v, gs, st, mask, dec)


def _dft_rows_kernel(f_ref, g_ref, ar_ref, ai_ref, *, n1):
    for t in range(DFT_ROWS):
        x = f_ref[:, t, :].astype(BF16)
        a = _dot(g_ref[t], x)
        ar_ref[:, t, :] = a[:n1]
        ai_ref[:, t, :] = a[n1:]


def _dft_rows(f, g):
    b, n1, n2, width = f.shape
    blk = pl.BlockSpec((None, n1, DFT_ROWS, width), lambda i, j: (i, 0, j, 0))
    out = jax.ShapeDtypeStruct(f.shape, F32)
    return pl.pallas_call(
        functools.partial(_dft_rows_kernel, n1=n1),
        out_shape=(out, out),
        grid=(b, n2 // DFT_ROWS),
        in_specs=[blk, pl.BlockSpec((DFT_ROWS, 2 * n1, n1), lambda i, j: (j, 0, 0))],
        out_specs=(blk, blk),
        compiler_params=_params(2),
        name="dft_rows",
    )(f, g)


def _dft_cols_kernel(ar_ref, ai_ref, m_ref, c_ref, o_ref, *, n2):
    for t in range(DFT_ROWS):
        a = jnp.concatenate([ar_ref[t], ai_ref[t]], axis=0).astype(BF16)
        y = _dot(m_ref[...], a)
        for grp in range(FFT_GROUPS):
            cols = slice(grp * FFT_GROUP_DIM, (grp + 1) * FFT_GROUP_DIM)
            y_both = jnp.concatenate([y[:n2, cols], y[n2:, cols]], axis=1).astype(BF16)
            o_ref[:, t, cols] = _dot(y_both, c_ref[...])


def _dft_cols(ar, ai, m, c):
    b, n1, n2, width = ar.shape
    in_blk = pl.BlockSpec((None, DFT_ROWS, n2, width), lambda i, j: (i, j, 0, 0))
    out_blk = pl.BlockSpec((None, n2, DFT_ROWS, width), lambda i, j: (i, 0, j, 0))
    return pl.pallas_call(
        functools.partial(_dft_cols_kernel, n2=n2),
        out_shape=jax.ShapeDtypeStruct((b, n2, n1, width), F32),
        grid=(b, n1 // DFT_ROWS),
        in_specs=[in_blk, in_blk, _resident(m.shape), _resident(c.shape)],
        out_specs=out_blk,
        compiler_params=_params(2),
        name="dft_cols",
    )(ar, ai, m, c)


def _dft_tables(s):
    n2 = DFT_N2
    n1 = s // n2
    two_pi = 2.0 * jnp.pi
    k1 = jnp.arange(n1, dtype=jnp.int32)[None, :, None]
    n = n2 * jnp.arange(n1, dtype=jnp.int32)[None, None, :] + jnp.arange(n2, dtype=jnp.int32)[:, None, None]
    ang = ((k1 * n) % s).astype(F32) * (two_pi / s)
    scale = (FFT_GROUP_DIM * s) ** -0.5
    g = jnp.concatenate([jnp.cos(ang), -jnp.sin(ang)], axis=1) * scale
    idx = jnp.arange(n2, dtype=jnp.int32)
    ang2 = ((idx[:, None] * idx[None, :]) % n2).astype(F32) * (two_pi / n2)
    c2, s2 = jnp.cos(ang2), jnp.sin(ang2)
    m = jnp.concatenate([jnp.concatenate([c2, s2], axis=1), jnp.concatenate([-s2, c2], axis=1)], axis=0)
    idc = jnp.arange(FFT_GROUP_DIM, dtype=jnp.int32)
    angc = ((idc[:, None] * idc[None, :]) % FFT_GROUP_DIM).astype(F32) * (two_pi / FFT_GROUP_DIM)
    c = jnp.concatenate([jnp.cos(angc), jnp.sin(angc)], axis=0)
    return g.astype(BF16), m.astype(BF16), c.astype(BF16)


def _mixer_out_kernel(x_ref, g_ref, ret_ref, fr_ref, wg_ref, wr_ref, wf_ref, wm_ref, o_ref, *, d):
    x = x_ref[...]
    xn = _rmsnorm(x, g_ref[...]).astype(BF16)
    ret = _dot(ret_ref[...], wr_ref[...])
    merged = jax.nn.sigmoid(_dot(xn, wg_ref[:, 0:d])) * ret
    fou = _dot(fr_ref[...].astype(BF16), wf_ref[...])
    merged = merged + jax.nn.sigmoid(_dot(xn, wg_ref[:, d:2 * d])) * fou
    o_ref[...] = x + _dot(merged.astype(BF16), wm_ref[...])


def _mixer_out(x, g, ret_in, fr, w_gates, w_ret, w_fft, w_mix):
    n_tok, d = x.shape
    tm = TOKEN_TILE
    tok = lambda width: pl.BlockSpec((tm, width), lambda i: (i, 0))
    return pl.pallas_call(
        functools.partial(_mixer_out_kernel, d=d),
        out_shape=jax.ShapeDtypeStruct((n_tok, d), F32),
        grid=(n_tok // tm,),
        in_specs=[tok(d), _resident((1, d)), tok(RET_WIDTH), tok(FFT_WIDTH), _resident(w_gates.shape),
                  _resident(w_ret.shape), _resident(w_fft.shape), _resident(w_mix.shape)],
        out_specs=tok(d),
        compiler_params=_params(1),
        name="mixer_out",
    )(x, g, ret_in, fr, w_gates, w_ret, w_fft, w_mix)


def _rotary_tables(s):
    half = HEAD_DIM // 2
    inv_freq = 1.0 / (ROPE_BASE ** jnp.linspace(0.0, 1.0, half, dtype=F32))
    ang = jnp.arange(s, dtype=F32)[:, None] * inv_freq[None, :]
    cos, sin = jnp.cos(ang), jnp.sin(ang)
    return jnp.concatenate([cos, cos], axis=1), jnp.concatenate([-sin, sin], axis=1)


def _layer_weights(l, g_ffn1, w_ffn1_in, w_ffn1_out, g_mix, w_in, w_ret_out, w_fft_out, w_mix_out,
                   g_ffn2, w_ffn2_in, w_ffn2_out):
    d = w_in.shape[1]
    perm = jnp.concatenate([jnp.arange(0, HEAD_DIM, 2), jnp.arange(1, HEAD_DIM, 2)])
    head_perm = (jnp.arange(RET_HEADS)[:, None] * HEAD_DIM + perm[None, :]).reshape(-1)
    w = w_in[l]
    w_proj = jnp.concatenate([w[:, 0:RET_WIDTH][:, head_perm], w[:, RET_WIDTH:2 * RET_WIDTH][:, head_perm],
                              w[:, 2 * RET_WIDTH:4 * RET_WIDTH + FFT_WIDTH]], axis=1).astype(BF16)
    return dict(
        g_ffn1=g_ffn1[l][None, :], w_ffn1_in=w_ffn1_in[l].astype(BF16), w_ffn1_out=w_ffn1_out[l].astype(BF16),
        g_mix=g_mix[l][None, :], w_proj=w_proj, w_gates=w[:, 4 * RET_WIDTH + FFT_WIDTH:].astype(BF16),
        w_ret_out=w_ret_out[l].astype(BF16), w_fft_out=w_fft_out[l].astype(BF16), w_mix_out=w_mix_out[l].astype(BF16),
        g_ffn2=g_ffn2[l][None, :], w_ffn2_in=w_ffn2_in[l].astype(BF16), w_ffn2_out=w_ffn2_out[l].astype(BF16),
    )


def _trunk(x, layers, decays, ret_tables, g_final):
    b, s, d = x.shape
    assert s % TOKEN_TILE == 0 and TOKEN_TILE % RET_CHUNK == 0 and s % (DFT_N2 * DFT_ROWS) == 0
    n1 = s // DFT_N2
    cos, sin = _rotary_tables(s)
    dft_g, dft_m, dft_c = _dft_tables(s)
    x = x.reshape(b * s, d)
    for l, w in enumerate(layers):
        ld_f, ld_b = decays[l]
        mask, dec = ret_tables[l]
        x = _ffn(x, w["g_ffn1"], w["w_ffn1_in"], w["w_ffn1_out"], g_final, final_norm=False)
        q, k, v, gs, f, kv = _mixer_in(x.reshape(b, s, d), w["g_mix"], w["w_proj"], cos, sin, ld_f, ld_b)
        st = _ret_states(kv, ld_f, ld_b)
        ret_in = _retention(q, k, v, gs, st, mask, dec, ld_f, ld_b)
        a_re, a_im = _dft_rows(f.reshape(b, n1, DFT_N2, FFT_WIDTH), dft_g)
        fr = _dft_cols(a_re, a_im, dft_m, dft_c)
        x = _mixer_out(x, w["g_mix"], ret_in.reshape(b * s, RET_WIDTH), fr.reshape(b * s, FFT_WIDTH),
                       w["w_gates"], w["w_ret_out"], w["w_fft_out"], w["w_mix_out"])
        x = _ffn(x, w["g_ffn2"], w["w_ffn2_in"], w["w_ffn2_out"], g_final, final_norm=(l == len(layers) - 1))
    return x.reshape(b, s, d)


def kernel(x_prompt, x_sample, g_ffn1, w_ffn1_in, w_ffn1_out, g_mix, w_in, ret_log_decay_fwd, ret_log_decay_bwd,
           w_ret_out, w_fft_out, w_mix_out, g_ffn2, w_ffn2_in, w_ffn2_out, g_final):
    depth = w_in.shape[0]
    layers = [_layer_weights(l, g_ffn1, w_ffn1_in, w_ffn1_out, g_mix, w_in, w_ret_out, w_fft_out, w_mix_out,
                             g_ffn2, w_ffn2_in, w_ffn2_out) for l in range(depth)]
    decays = [(ret_log_decay_fwd[l], ret_log_decay_bwd[l]) for l in range(depth)]
    ret_tables = [_ret_decay(ld_f, ld_b, RET_CHUNK) for ld_f, ld_b in decays]
    g_fin = g_final[None, :]
    return (_trunk(x_prompt, layers, decays, ret_tables, g_fin), _trunk(x_sample, layers, decays, ret_tables, g_fin))
```

```python
import functools

import jax
import jax.numpy as jnp
from jax import lax
from jax.experimental import pallas as pl
from jax.experimental.pallas import tpu as pltpu

F32 = jnp.float32
BF16 = jnp.bfloat16

RET_HEADS = 4
HEAD_DIM = 128
RET_WIDTH = RET_HEADS * HEAD_DIM
FFT_GROUPS = 4
FFT_GROUP_DIM = 128
FFT_WIDTH = FFT_GROUPS * FFT_GROUP_DIM
RMS_EPS = 1e-6
GN_EPS = 1e-6
ROPE_BASE = 10000.0

V7X_LANES = 128
V7X_SUBLANES_F32 = 8
V7X_MXU_COLUMNS = 256
V7X_VMEM_BYTES = 64 * 1024 * 1024
VMEM_LIMIT_BYTES = V7X_VMEM_BYTES - 8 * 1024 * 1024

TOKEN_TILE = 1024
RET_CHUNK = 512
FF_CHUNK = V7X_MXU_COLUMNS
DFT_N2 = 128
DFT_ROWS = V7X_SUBLANES_F32


def _params(n_grid_dims):
    return pltpu.CompilerParams(
        dimension_semantics=("parallel",) * n_grid_dims,
        vmem_limit_bytes=VMEM_LIMIT_BYTES,
    )


def _resident(shape):
    return pl.BlockSpec(shape, lambda *_: (0,) * len(shape), pipeline_mode=pl.Buffered(1))


def _layer_resident(shape, layer):
    index = (layer,) + (0,) * (len(shape) - 1)
    return pl.BlockSpec((None,) + tuple(shape[1:]), lambda *_: index, pipeline_mode=pl.Buffered(1))


def _smem():
    return pl.BlockSpec(memory_space=pltpu.SMEM)


def _rms_scale(x):
    return lax.rsqrt(jnp.mean(x * x, axis=-1, keepdims=True) + RMS_EPS)


def _rmsnorm(x, g):
    return x * _rms_scale(x) * g


def _dot(a, b):
    return jnp.dot(a, b, preferred_element_type=F32)


def _ffn_kernel(x_ref, g_ref, win_ref, wout_ref, gfin_ref, o_ref, h_ref, *, d_ff, final_norm):
    x = x_ref[...]
    xg = (x * g_ref[...]).astype(BF16)
    scale = _rms_scale(x)
    for lo in range(0, d_ff, FF_CHUNK):
        a = _dot(xg, win_ref[:, lo:lo + FF_CHUNK]) * scale
        b = _dot(xg, win_ref[:, d_ff + lo:d_ff + lo + FF_CHUNK]) * scale
        h_ref[:, lo:lo + FF_CHUNK] = (jax.nn.silu(a) * b).astype(BF16)
    out = x + 0.5 * _dot(h_ref[...], wout_ref[...])
    if final_norm:
        out = _rmsnorm(out, gfin_ref[...])
    o_ref[...] = out


def _ffn(x, g, w_in, w_out, g_final, layer, *, final_norm):
    n_tok, d = x.shape
    d_ff = w_out.shape[1]
    tm = TOKEN_TILE
    assert n_tok % tm == 0 and d_ff % FF_CHUNK == 0
    tile = pl.BlockSpec((tm, d), lambda i: (i, 0))
    return pl.pallas_call(
        functools.partial(_ffn_kernel, d_ff=d_ff, final_norm=final_norm),
        out_shape=jax.ShapeDtypeStruct((n_tok, d), F32),
        grid=(n_tok // tm,),
        in_specs=[tile, _layer_resident(g.shape, layer), _layer_resident(w_in.shape, layer),
                  _layer_resident(w_out.shape, layer), _layer_resident(g_final.shape, 0)],
        out_specs=tile,
        scratch_shapes=[pltpu.VMEM((tm, d_ff), BF16)],
        compiler_params=_params(1),
        name="ffn",
    )(x, g, w_in, w_out, g_final)


def _mixer_in_kernel(ldf_ref, ldb_ref, x_ref, g_ref, w_ref, cos_ref, sin_ref,
                     q_ref, k_ref, v_ref, gs_ref, f_ref, kv_ref, *, tm, layer):
    x = x_ref[...]
    xg = (x * g_ref[...]).astype(BF16)
    scale = _rms_scale(x)
    cos = cos_ref[...]
    sin = sin_ref[...]
    q = _dot(xg, w_ref[:, 0:RET_WIDTH]) * scale
    k = _dot(xg, w_ref[:, RET_WIDTH:2 * RET_WIDTH]) * scale
    v = (_dot(xg, w_ref[:, 2 * RET_WIDTH:3 * RET_WIDTH]) * scale).astype(BF16)
    v_ref[...] = v
    gs_ref[...] = _dot(xg, w_ref[:, 3 * RET_WIDTH:4 * RET_WIDTH]) * scale
    f_ref[...] = _dot(xg, w_ref[:, 4 * RET_WIDTH:4 * RET_WIDTH + FFT_WIDTH]) * scale
    row = lax.broadcasted_iota(jnp.int32, (tm, HEAD_DIM), 0).astype(F32)
    k_scale = HEAD_DIM ** -0.5
    for h in range(RET_HEADS):
        cols = slice(h * HEAD_DIM, (h + 1) * HEAD_DIM)
        qh = q[:, cols]
        q_ref[:, cols] = (qh * cos + pltpu.roll(qh, HEAD_DIM // 2, axis=1) * sin).astype(BF16)
        kh = k[:, cols]
        kr = (kh * cos + pltpu.roll(kh, HEAD_DIM // 2, axis=1) * sin) * k_scale
        k_ref[:, cols] = kr.astype(BF16)
        vh = v[:, cols]
        kf = (kr * jnp.exp(ldf_ref[layer, h] * (tm - 1.0 - row))).astype(BF16)
        kb = (kr * jnp.exp(ldb_ref[layer, h] * row)).astype(BF16)
        tn = (((0,), (0,)), ((), ()))
        kv_ref[2 * h] = lax.dot_general(kf, vh, tn, preferred_element_type=F32)
        kv_ref[2 * h + 1] = lax.dot_general(kb, vh, tn, preferred_element_type=F32)


def _mixer_in(x, g, w, cos, sin, ld_f, ld_b, layer):
    b, s, d = x.shape
    tm = TOKEN_TILE
    nt = s // tm
    tok = lambda width: pl.BlockSpec((None, tm, width), lambda i, j: (i, j, 0))
    rot = pl.BlockSpec((tm, HEAD_DIM), lambda i, j: (j, 0))
    kv_spec = pl.BlockSpec((None, None, 2 * RET_HEADS, HEAD_DIM, HEAD_DIM), lambda i, j: (i, j, 0, 0, 0))
    act = lambda dtype: jax.ShapeDtypeStruct((b, s, RET_WIDTH), dtype)
    return pl.pallas_call(
        functools.partial(_mixer_in_kernel, tm=tm, layer=layer),
        out_shape=(act(BF16), act(BF16), act(BF16), act(F32), act(F32),
                   jax.ShapeDtypeStruct((b, nt, 2 * RET_HEADS, HEAD_DIM, HEAD_DIM), F32)),
        grid=(b, nt),
        in_specs=[_smem(), _smem(), tok(d), _layer_resident(g.shape, layer), _layer_resident(w.shape, layer),
                  rot, rot],
        out_specs=(tok(RET_WIDTH), tok(RET_WIDTH), tok(RET_WIDTH), tok(RET_WIDTH), tok(FFT_WIDTH), kv_spec),
        compiler_params=_params(2),
        name="mixer_in",
    )(ld_f, ld_b, x, g, w, cos, sin)


def _ret_states_kernel(ldf_ref, ldb_ref, kv_ref, st_ref, *, nt, tm, layer):
    for h in range(RET_HEADS):
        a_f = jnp.exp(jnp.full((HEAD_DIM, HEAD_DIM), ldf_ref[layer, h] * tm, F32))
        a_b = jnp.exp(jnp.full((HEAD_DIM, HEAD_DIM), ldb_ref[layer, h] * tm, F32))
        state = jnp.zeros((HEAD_DIM, HEAD_DIM), F32)
        for t in range(nt):
            st_ref[t, 2 * h] = state
            state = a_f * state + kv_ref[t, 2 * h]
        state = jnp.zeros((HEAD_DIM, HEAD_DIM), F32)
        for t in reversed(range(nt)):
            st_ref[t, 2 * h + 1] = state
            state = a_b * state + kv_ref[t, 2 * h + 1]


def _ret_states(kv, ld_f, ld_b, layer):
    b, nt = kv.shape[:2]
    spec = pl.BlockSpec((None,) + kv.shape[1:], lambda i: (i, 0, 0, 0, 0))
    return pl.pallas_call(
        functools.partial(_ret_states_kernel, nt=nt, tm=TOKEN_TILE, layer=layer),
        out_shape=jax.ShapeDtypeStruct(kv.shape, F32),
        grid=(b,),
        in_specs=[_smem(), _smem(), spec],
        out_specs=spec,
        compiler_params=_params(1),
        name="ret_states",
    )(ld_f, ld_b, kv)


def _ret_decay_kernel(ldf_ref, ldb_ref, mask_ref, dec_ref, *, rc):
    layer = pl.program_id(0)
    i = lax.broadcasted_iota(jnp.int32, (rc, rc), 0)
    j = lax.broadcasted_iota(jnp.int32, (rc, rc), 1)
    dist = (i - j).astype(F32)
    row = lax.broadcasted_iota(jnp.int32, (rc, HEAD_DIM), 0).astype(F32)
    for h in range(RET_HEADS):
        ld_f = ldf_ref[layer, h]
        ld_b = ldb_ref[layer, h]
        mask_ref[h] = jnp.exp(jnp.where(i >= j, ld_f * dist, -ld_b * dist))
        dec_ref[h, 0] = jnp.exp(ld_f * (row + 1.0))
        dec_ref[h, 1] = jnp.exp(ld_b * (rc - row))
        dec_ref[h, 2] = jnp.exp(ld_f * (rc - 1.0 - row))
        dec_ref[h, 3] = jnp.exp(ld_b * row)


def _ret_decay(ld_f, ld_b, rc):
    depth = ld_f.shape[0]
    return pl.pallas_call(
        functools.partial(_ret_decay_kernel, rc=rc),
        out_shape=(jax.ShapeDtypeStruct((depth, RET_HEADS, rc, rc), F32),
                   jax.ShapeDtypeStruct((depth, RET_HEADS, 4, rc, HEAD_DIM), F32)),
        grid=(depth,),
        in_specs=[_smem(), _smem()],
        out_specs=(pl.BlockSpec((None, RET_HEADS, rc, rc), lambda l: (l, 0, 0, 0)),
                   pl.BlockSpec((None, RET_HEADS, 4, rc, HEAD_DIM), lambda l: (l, 0, 0, 0, 0))),
        compiler_params=_params(1),
        name="ret_decay",
    )(ld_f, ld_b)


def _retention_kernel(ldf_ref, ldb_ref, q_ref, k_ref, v_ref, gs_ref, st_ref, mask_ref, dec_ref, o_ref,
                      *, tm, rc, layer):
    nc = tm // rc
    tn = (((0,), (0,)), ((), ()))
    nt_dims = (((1,), (1,)), ((), ()))
    for h in range(RET_HEADS):
        cols = slice(h * HEAD_DIM, (h + 1) * HEAD_DIM)
        q_dec_f, q_dec_b, k_dec_f, k_dec_b = (dec_ref[h, n] for n in range(4))
        rows = [slice(c * rc, (c + 1) * rc) for c in range(nc)]
        fwd = [st_ref[2 * h]]
        bwd = [st_ref[2 * h + 1]]
        if nc > 1:
            a_f = jnp.exp(jnp.full((HEAD_DIM, HEAD_DIM), ldf_ref[layer, h] * rc, F32))
            a_b = jnp.exp(jnp.full((HEAD_DIM, HEAD_DIM), ldb_ref[layer, h] * rc, F32))
            for c in range(nc - 1):
                kc = k_ref[rows[c], cols].astype(F32)
                kv = lax.dot_general((kc * k_dec_f).astype(BF16), v_ref[rows[c], cols], tn, preferred_element_type=F32)
                fwd.append(a_f * fwd[-1] + kv)
            for c in range(nc - 1, 0, -1):
                kc = k_ref[rows[c], cols].astype(F32)
                kv = lax.dot_general((kc * k_dec_b).astype(BF16), v_ref[rows[c], cols], tn, preferred_element_type=F32)
                bwd.append(a_b * bwd[-1] + kv)
            bwd = bwd[::-1]
        for c in range(nc):
            qc = q_ref[rows[c], cols]
            kc = k_ref[rows[c], cols]
            vc = v_ref[rows[c], cols]
            scores = lax.dot_general(qc, kc, nt_dims, preferred_element_type=F32)
            intra = _dot((scores * mask_ref[h]).astype(BF16), vc)
            qf = qc.astype(F32)
            q_both = jnp.concatenate([(qf * q_dec_f).astype(BF16), (qf * q_dec_b).astype(BF16)], axis=1)
            s_both = jnp.concatenate([fwd[c], bwd[c]], axis=0).astype(BF16)
            y = intra + _dot(q_both, s_both)
            yc = y - jnp.mean(y, axis=-1, keepdims=True)
            yn = yc * lax.rsqrt(jnp.mean(yc * yc, axis=-1, keepdims=True) + GN_EPS)
            o_ref[rows[c], cols] = (jax.nn.silu(gs_ref[rows[c], cols]) * yn).astype(BF16)


def _retention(q, k, v, gs, st, mask, dec, ld_f, ld_b, layer):
    b, s, _ = q.shape
    tm = TOKEN_TILE
    rc = mask.shape[-1]
    tok = pl.BlockSpec((None, tm, RET_WIDTH), lambda i, j: (i, j, 0))
    st_spec = pl.BlockSpec((None, None, 2 * RET_HEADS, HEAD_DIM, HEAD_DIM), lambda i, j: (i, j, 0, 0, 0))
    return pl.pallas_call(
        functools.partial(_retention_kernel, tm=tm, rc=rc, layer=layer),
        out_shape=jax.ShapeDtypeStruct((b, s, RET_WIDTH), BF16),
        grid=(b, s // tm),
        in_specs=[_smem(), _smem(), tok, tok, tok, tok, st_spec, _layer_resident(mask.shape, layer),
                  _layer_resident(dec.shape, layer)],
        out_specs=tok,
        compiler_params=_params(2),
        name="retention",
    )(ld_f, ld_b, q, k, v, gs, st, mask, dec)


def _dft_rows_kernel(*refs, n1):
    f_refs, g_ref, ar_ref, ai_ref, flat_ref = refs[:FFT_GROUPS], refs[FFT_GROUPS], refs[-3], refs[-2], refs[-1]
    for grp, f_ref in enumerate(f_refs):
        flat_ref[grp] = f_ref[...].reshape(n1 * DFT_ROWS, FFT_GROUP_DIM)
    for t in range(DFT_ROWS):
        x = jnp.concatenate([flat_ref[grp, pl.ds(t, n1, stride=DFT_ROWS), :] for grp in range(FFT_GROUPS)],
                            axis=1).astype(BF16)
        a = _dot(g_ref[t], x)
        ar_ref[:, t, :] = a[:n1]
        ai_ref[:, t, :] = a[n1:]


def _dft_rows(f, g):
    b, n1, n2, width = f.shape
    blk = pl.BlockSpec((None, n1, DFT_ROWS, width), lambda i, j: (i, 0, j, 0))
    grp = [pl.BlockSpec((None, n1, DFT_ROWS, FFT_GROUP_DIM), functools.partial(lambda i, j, c: (i, 0, j, c), c=c))
           for c in range(FFT_GROUPS)]
    out = jax.ShapeDtypeStruct(f.shape, F32)
    return pl.pallas_call(
        functools.partial(_dft_rows_kernel, n1=n1),
        out_shape=(out, out),
        grid=(b, n2 // DFT_ROWS),
        in_specs=grp + [pl.BlockSpec((DFT_ROWS, 2 * n1, n1), lambda i, j: (j, 0, 0))],
        out_specs=(blk, blk),
        scratch_shapes=[pltpu.VMEM((FFT_GROUPS, n1 * DFT_ROWS, FFT_GROUP_DIM), F32)],
        compiler_params=_params(2),
        name="dft_rows",
    )(*([f] * FFT_GROUPS), g)


def _dft_cols_kernel(ar_ref, ai_ref, m_ref, c_ref, o_ref, *, n2):
    for t in range(DFT_ROWS):
        a = jnp.concatenate([ar_ref[t], ai_ref[t]], axis=0).astype(BF16)
        y = _dot(m_ref[...], a)
        for grp in range(FFT_GROUPS):
            cols = slice(grp * FFT_GROUP_DIM, (grp + 1) * FFT_GROUP_DIM)
            y_both = jnp.concatenate([y[:n2, cols], y[n2:, cols]], axis=1).astype(BF16)
            o_ref[:, t, cols] = _dot(y_both, c_ref[...])


def _dft_cols(ar, ai, m, c):
    b, n1, n2, width = ar.shape
    in_blk = pl.BlockSpec((None, DFT_ROWS, n2, width), lambda i, j: (i, j, 0, 0))
    out_blk = pl.BlockSpec((None, n2, DFT_ROWS, width), lambda i, j: (i, 0, j, 0))
    return pl.pallas_call(
        functools.partial(_dft_cols_kernel, n2=n2),
        out_shape=jax.ShapeDtypeStruct((b, n2, n1, width), F32),
        grid=(b, n1 // DFT_ROWS),
        in_specs=[in_blk, in_blk, _resident(m.shape), _resident(c.shape)],
        out_specs=out_blk,
        compiler_params=_params(2),
        name="dft_cols",
    )(ar, ai, m, c)


def _dft_tables(s):
    n2 = DFT_N2
    n1 = s // n2
    two_pi = 2.0 * jnp.pi
    k1 = jnp.arange(n1, dtype=jnp.int32)[None, :, None]
    n = n2 * jnp.arange(n1, dtype=jnp.int32)[None, None, :] + jnp.arange(n2, dtype=jnp.int32)[:, None, None]
    ang = ((k1 * n) % s).astype(F32) * (two_pi / s)
    scale = (FFT_GROUP_DIM * s) ** -0.5
    g = jnp.concatenate([jnp.cos(ang), -jnp.sin(ang)], axis=1) * scale
    idx = jnp.arange(n2, dtype=jnp.int32)
    ang2 = ((idx[:, None] * idx[None, :]) % n2).astype(F32) * (two_pi / n2)
    c2, s2 = jnp.cos(ang2), jnp.sin(ang2)
    m = jnp.concatenate([jnp.concatenate([c2, s2], axis=1), jnp.concatenate([-s2, c2], axis=1)], axis=0)
    idc = jnp.arange(FFT_GROUP_DIM, dtype=jnp.int32)
    angc = ((idc[:, None] * idc[None, :]) % FFT_GROUP_DIM).astype(F32) * (two_pi / FFT_GROUP_DIM)
    c = jnp.concatenate([jnp.cos(angc), jnp.sin(angc)], axis=0)
    return g.astype(BF16), m.astype(BF16), c.astype(BF16)


def _mixer_out_kernel(x_ref, g_ref, ret_ref, fr_ref, wg_ref, wr_ref, wf_ref, wm_ref, o_ref, *, d):
    x = x_ref[...]
    xg = (x * g_ref[...]).astype(BF16)
    scale = _rms_scale(x)
    ret = _dot(ret_ref[...], wr_ref[...])
    merged = jax.nn.sigmoid(_dot(xg, wg_ref[:, 0:d]) * scale) * ret
    fou = _dot(fr_ref[...].astype(BF16), wf_ref[...])
    merged = merged + jax.nn.sigmoid(_dot(xg, wg_ref[:, d:2 * d]) * scale) * fou
    o_ref[...] = x + _dot(merged.astype(BF16), wm_ref[...])


def _mixer_out(x, g, ret_in, fr, w_gates, w_ret, w_fft, w_mix, layer):
    n_tok, d = x.shape
    tm = TOKEN_TILE
    tok = lambda width: pl.BlockSpec((tm, width), lambda i: (i, 0))
    return pl.pallas_call(
        functools.partial(_mixer_out_kernel, d=d),
        out_shape=jax.ShapeDtypeStruct((n_tok, d), F32),
        grid=(n_tok // tm,),
        in_specs=[tok(d), _layer_resident(g.shape, layer), tok(RET_WIDTH), tok(FFT_WIDTH),
                  _layer_resident(w_gates.shape, layer), _layer_resident(w_ret.shape, layer),
                  _layer_resident(w_fft.shape, layer), _layer_resident(w_mix.shape, layer)],
        out_specs=tok(d),
        compiler_params=_params(1),
        name="mixer_out",
    )(x, g, ret_in, fr, w_gates, w_ret, w_fft, w_mix)


def _rotary_tables(s):
    half = HEAD_DIM // 2
    inv_freq = 1.0 / (ROPE_BASE ** jnp.linspace(0.0, 1.0, half, dtype=F32))
    ang = jnp.arange(s, dtype=F32)[:, None] * inv_freq[None, :]
    cos, sin = jnp.cos(ang), jnp.sin(ang)
    return jnp.concatenate([cos, cos], axis=1), jnp.concatenate([-sin, sin], axis=1)


def _prepare_weights(g_ffn1, w_ffn1_in, w_ffn1_out, g_mix, w_in, w_ret_out, w_fft_out, w_mix_out,
                     g_ffn2, w_ffn2_in, w_ffn2_out, g_final):
    perm = jnp.concatenate([jnp.arange(0, HEAD_DIM, 2), jnp.arange(1, HEAD_DIM, 2)])
    qk_cols = (jnp.arange(2 * RET_HEADS)[:, None] * HEAD_DIM + perm[None, :]).reshape(-1)
    n_proj = 4 * RET_WIDTH + FFT_WIDTH
    proj_cols = jnp.concatenate([qk_cols, jnp.arange(2 * RET_WIDTH, n_proj)])
    gain = lambda g: g[:, None, :]
    return dict(
        g_ffn1=gain(g_ffn1), w_ffn1_in=w_ffn1_in.astype(BF16), w_ffn1_out=w_ffn1_out.astype(BF16),
        g_mix=gain(g_mix), w_proj=w_in[:, :, proj_cols].astype(BF16), w_gates=w_in[:, :, n_proj:].astype(BF16),
        w_ret_out=w_ret_out.astype(BF16), w_fft_out=w_fft_out.astype(BF16), w_mix_out=w_mix_out.astype(BF16),
        g_ffn2=gain(g_ffn2), w_ffn2_in=w_ffn2_in.astype(BF16), w_ffn2_out=w_ffn2_out.astype(BF16),
        g_final=g_final[None, None, :],
    )


def _trunk(x, w, ld_f, ld_b, mask, dec):
    b, s, d = x.shape
    assert s % TOKEN_TILE == 0 and TOKEN_TILE % RET_CHUNK == 0 and s % (DFT_N2 * DFT_ROWS) == 0
    n1 = s // DFT_N2
    depth = ld_f.shape[0]
    cos, sin = _rotary_tables(s)
    dft_g, dft_m, dft_c = _dft_tables(s)
    x = x.reshape(b * s, d)
    for l in range(depth):
        x = _ffn(x, w["g_ffn1"], w["w_ffn1_in"], w["w_ffn1_out"], w["g_final"], l, final_norm=False)
        q, k, v, gs, f, kv = _mixer_in(x.reshape(b, s, d), w["g_mix"], w["w_proj"], cos, sin, ld_f, ld_b, l)
        st = _ret_states(kv, ld_f, ld_b, l)
        ret_in = _retention(q, k, v, gs, st, mask, dec, ld_f, ld_b, l)
        a_re, a_im = _dft_rows(f.reshape(b, n1, DFT_N2, FFT_WIDTH), dft_g)
        fr = _dft_cols(a_re, a_im, dft_m, dft_c)
        x = _mixer_out(x, w["g_mix"], ret_in.reshape(b * s, RET_WIDTH), fr.reshape(b * s, FFT_WIDTH),
                       w["w_gates"], w["w_ret_out"], w["w_fft_out"], w["w_mix_out"], l)
        x = _ffn(x, w["g_ffn2"], w["w_ffn2_in"], w["w_ffn2_out"], w["g_final"], l, final_norm=(l == depth - 1))
    return x.reshape(b, s, d)


def kernel(x_prompt, x_sample, g_ffn1, w_ffn1_in, w_ffn1_out, g_mix, w_in, ret_log_decay_fwd, ret_log_decay_bwd,
           w_ret_out, w_fft_out, w_mix_out, g_ffn2, w_ffn2_in, w_ffn2_out, g_final):
    w = _prepare_weights(g_ffn1, w_ffn1_in, w_ffn1_out, g_mix, w_in, w_ret_out, w_fft_out, w_mix_out,
                         g_ffn2, w_ffn2_in, w_ffn2_out, g_final)
    mask, dec = _ret_decay(ret_log_decay_fwd, ret_log_decay_bwd, RET_CHUNK)
    return tuple(_trunk(x, w, ret_log_decay_fwd, ret_log_decay_bwd, mask, dec) for x in (x_prompt, x_sample))
```

```python
import functools

import jax
import jax.numpy as jnp
from jax import lax
from jax.experimental import pallas as pl
from jax.experimental.pallas import tpu as pltpu

F32 = jnp.float32
BF16 = jnp.bfloat16

RET_HEADS = 4
HEAD_DIM = 128
RET_WIDTH = RET_HEADS * HEAD_DIM
FFT_GROUPS = 4
FFT_GROUP_DIM = 128
FFT_WIDTH = FFT_GROUPS * FFT_GROUP_DIM
RMS_EPS = 1e-6
GN_EPS = 1e-6
ROPE_BASE = 10000.0

V7X_LANES = 128
V7X_SUBLANES_F32 = 8
V7X_SUBLANES_BF16 = 16
V7X_MXU_COLUMNS = 256
V7X_VMEM_BYTES = 64 * 1024 * 1024
VMEM_LIMIT_BYTES = V7X_VMEM_BYTES - 8 * 1024 * 1024

TOKEN_TILE = 1024
RET_CHUNK = 512
FF_CHUNK = V7X_MXU_COLUMNS
DFT_N2 = 128
DFT_ROWS = V7X_SUBLANES_BF16


def _params(n_grid_dims):
    return pltpu.CompilerParams(
        dimension_semantics=("parallel",) * n_grid_dims,
        vmem_limit_bytes=VMEM_LIMIT_BYTES,
    )


def _resident(shape):
    return pl.BlockSpec(shape, lambda *_: (0,) * len(shape), pipeline_mode=pl.Buffered(1))


def _layer_resident(shape, layer):
    index = (layer,) + (0,) * (len(shape) - 1)
    return pl.BlockSpec((None,) + tuple(shape[1:]), lambda *_: index, pipeline_mode=pl.Buffered(1))


def _smem():
    return pl.BlockSpec(memory_space=pltpu.SMEM)


def _rms_scale(x):
    return lax.rsqrt(jnp.mean(x * x, axis=-1, keepdims=True) + RMS_EPS)


def _rmsnorm(x, g):
    return x * _rms_scale(x) * g


def _dot(a, b):
    return jnp.dot(a, b, preferred_element_type=F32)


def _ffn_kernel(x_ref, g_ref, win_ref, wout_ref, gfin_ref, o_ref, h_ref, *, d_ff, final_norm):
    x = x_ref[...]
    xg = (x * g_ref[...]).astype(BF16)
    scale = _rms_scale(x)
    for lo in range(0, d_ff, FF_CHUNK):
        a = _dot(xg, win_ref[:, lo:lo + FF_CHUNK]) * scale
        b = _dot(xg, win_ref[:, d_ff + lo:d_ff + lo + FF_CHUNK]) * scale
        h_ref[:, lo:lo + FF_CHUNK] = (jax.nn.silu(a) * b).astype(BF16)
    out = x + 0.5 * _dot(h_ref[...], wout_ref[...])
    if final_norm:
        out = _rmsnorm(out, gfin_ref[...])
    o_ref[...] = out


def _ffn(x, g, w_in, w_out, g_final, layer, *, final_norm):
    n_tok, d = x.shape
    d_ff = w_out.shape[1]
    tm = TOKEN_TILE
    assert n_tok % tm == 0 and d_ff % FF_CHUNK == 0
    tile = pl.BlockSpec((tm, d), lambda i: (i, 0))
    return pl.pallas_call(
        functools.partial(_ffn_kernel, d_ff=d_ff, final_norm=final_norm),
        out_shape=jax.ShapeDtypeStruct((n_tok, d), F32),
        grid=(n_tok // tm,),
        in_specs=[tile, _layer_resident(g.shape, layer), _layer_resident(w_in.shape, layer),
                  _layer_resident(w_out.shape, layer), _layer_resident(g_final.shape, 0)],
        out_specs=tile,
        scratch_shapes=[pltpu.VMEM((tm, d_ff), BF16)],
        compiler_params=_params(1),
        name="ffn",
    )(x, g, w_in, w_out, g_final)


def _mixer_in_kernel(ldf_ref, ldb_ref, x_ref, g_ref, w_ref, base_ref, rot_ref,
                     q_ref, k_ref, v_ref, gs_ref, f_ref, kv_ref, *, tm, layer):
    x = x_ref[...]
    xg = (x * g_ref[...]).astype(BF16)
    scale = _rms_scale(x)
    cos_b, sin_b = base_ref[0, 0:1, :], base_ref[1, 0:1, :]
    cos = cos_b * rot_ref[0] - sin_b * rot_ref[1]
    sin = sin_b * rot_ref[2] + cos_b * rot_ref[3]
    q = _dot(xg, w_ref[:, 0:RET_WIDTH]) * scale
    k = _dot(xg, w_ref[:, RET_WIDTH:2 * RET_WIDTH]) * scale
    v = (_dot(xg, w_ref[:, 2 * RET_WIDTH:3 * RET_WIDTH]) * scale).astype(BF16)
    v_ref[...] = v
    gs_ref[...] = _dot(xg, w_ref[:, 3 * RET_WIDTH:4 * RET_WIDTH]) * scale
    f_ref[...] = (_dot(xg, w_ref[:, 4 * RET_WIDTH:4 * RET_WIDTH + FFT_WIDTH]) * scale).astype(BF16)
    row = lax.broadcasted_iota(jnp.int32, (tm, HEAD_DIM), 0).astype(F32)
    k_scale = HEAD_DIM ** -0.5
    for h in range(RET_HEADS):
        cols = slice(h * HEAD_DIM, (h + 1) * HEAD_DIM)
        qh = q[:, cols]
        q_ref[:, cols] = (qh * cos + pltpu.roll(qh, HEAD_DIM // 2, axis=1) * sin).astype(BF16)
        kh = k[:, cols]
        kr = (kh * cos + pltpu.roll(kh, HEAD_DIM // 2, axis=1) * sin) * k_scale
        k_ref[:, cols] = kr.astype(BF16)
        vh = v[:, cols]
        kf = (kr * jnp.exp(ldf_ref[layer, h] * (tm - 1.0 - row))).astype(BF16)
        kb = (kr * jnp.exp(ldb_ref[layer, h] * row)).astype(BF16)
        tn = (((0,), (0,)), ((), ()))
        kv_ref[2 * h] = lax.dot_general(kf, vh, tn, preferred_element_type=F32)
        kv_ref[2 * h + 1] = lax.dot_general(kb, vh, tn, preferred_element_type=F32)


def _mixer_in(x, g, w, rot_base, rot_rows, ld_f, ld_b, layer):
    b, s, d = x.shape
    tm = TOKEN_TILE
    nt = s // tm
    tok = lambda width: pl.BlockSpec((None, tm, width), lambda i, j: (i, j, 0))
    base = pl.BlockSpec((None,) + rot_base.shape[1:], lambda i, j: (j, 0, 0, 0))
    kv_spec = pl.BlockSpec((None, None, 2 * RET_HEADS, HEAD_DIM, HEAD_DIM), lambda i, j: (i, j, 0, 0, 0))
    act = lambda dtype: jax.ShapeDtypeStruct((b, s, RET_WIDTH), dtype)
    return pl.pallas_call(
        functools.partial(_mixer_in_kernel, tm=tm, layer=layer),
        out_shape=(act(BF16), act(BF16), act(BF16), act(F32), act(BF16),
                   jax.ShapeDtypeStruct((b, nt, 2 * RET_HEADS, HEAD_DIM, HEAD_DIM), F32)),
        grid=(b, nt),
        in_specs=[_smem(), _smem(), tok(d), _layer_resident(g.shape, layer), _layer_resident(w.shape, layer),
                  base, _resident(rot_rows.shape)],
        out_specs=(tok(RET_WIDTH), tok(RET_WIDTH), tok(RET_WIDTH), tok(RET_WIDTH), tok(FFT_WIDTH), kv_spec),
        compiler_params=_params(2),
        name="mixer_in",
    )(ld_f, ld_b, x, g, w, rot_base, rot_rows)


def _ret_states_kernel(ldf_ref, ldb_ref, kv_ref, st_ref, *, nt, tm, layer):
    for h in range(RET_HEADS):
        a_f = jnp.exp(jnp.full((HEAD_DIM, HEAD_DIM), ldf_ref[layer, h] * tm, F32))
        a_b = jnp.exp(jnp.full((HEAD_DIM, HEAD_DIM), ldb_ref[layer, h] * tm, F32))
        state = jnp.zeros((HEAD_DIM, HEAD_DIM), F32)
        for t in range(nt):
            st_ref[t, 2 * h] = state
            state = a_f * state + kv_ref[t, 2 * h]
        state = jnp.zeros((HEAD_DIM, HEAD_DIM), F32)
        for t in reversed(range(nt)):
            st_ref[t, 2 * h + 1] = state
            state = a_b * state + kv_ref[t, 2 * h + 1]


def _ret_states(kv, ld_f, ld_b, layer):
    b, nt = kv.shape[:2]
    spec = pl.BlockSpec((None,) + kv.shape[1:], lambda i: (i, 0, 0, 0, 0))
    return pl.pallas_call(
        functools.partial(_ret_states_kernel, nt=nt, tm=TOKEN_TILE, layer=layer),
        out_shape=jax.ShapeDtypeStruct(kv.shape, F32),
        grid=(b,),
        in_specs=[_smem(), _smem(), spec],
        out_specs=spec,
        compiler_params=_params(1),
        name="ret_states",
    )(ld_f, ld_b, kv)


def _ret_decay_kernel(ldf_ref, ldb_ref, mask_ref, dec_ref, *, rc):
    layer = pl.program_id(0)
    i = lax.broadcasted_iota(jnp.int32, (rc, rc), 0)
    j = lax.broadcasted_iota(jnp.int32, (rc, rc), 1)
    dist = (i - j).astype(F32)
    row = lax.broadcasted_iota(jnp.int32, (rc, HEAD_DIM), 0).astype(F32)
    for h in range(RET_HEADS):
        ld_f = ldf_ref[layer, h]
        ld_b = ldb_ref[layer, h]
        mask_ref[h] = jnp.exp(jnp.where(i >= j, ld_f * dist, -ld_b * dist))
        dec_ref[h, 0] = jnp.exp(ld_f * (row + 1.0))
        dec_ref[h, 1] = jnp.exp(ld_b * (rc - row))
        dec_ref[h, 2] = jnp.exp(ld_f * (rc - 1.0 - row))
        dec_ref[h, 3] = jnp.exp(ld_b * row)


def _ret_decay(ld_f, ld_b, rc):
    depth = ld_f.shape[0]
    return pl.pallas_call(
        functools.partial(_ret_decay_kernel, rc=rc),
        out_shape=(jax.ShapeDtypeStruct((depth, RET_HEADS, rc, rc), F32),
                   jax.ShapeDtypeStruct((depth, RET_HEADS, 4, rc, HEAD_DIM), F32)),
        grid=(depth,),
        in_specs=[_smem(), _smem()],
        out_specs=(pl.BlockSpec((None, RET_HEADS, rc, rc), lambda l: (l, 0, 0, 0)),
                   pl.BlockSpec((None, RET_HEADS, 4, rc, HEAD_DIM), lambda l: (l, 0, 0, 0, 0))),
        compiler_params=_params(1),
        name="ret_decay",
    )(ld_f, ld_b)


def _retention_kernel(ldf_ref, ldb_ref, q_ref, k_ref, v_ref, gs_ref, st_ref, mask_ref, dec_ref, o_ref,
                      *, tm, rc, layer):
    nc = tm // rc
    tn = (((0,), (0,)), ((), ()))
    nt_dims = (((1,), (1,)), ((), ()))
    for h in range(RET_HEADS):
        cols = slice(h * HEAD_DIM, (h + 1) * HEAD_DIM)
        q_dec_f, q_dec_b, k_dec_f, k_dec_b = (dec_ref[h, n] for n in range(4))
        rows = [slice(c * rc, (c + 1) * rc) for c in range(nc)]
        fwd = [st_ref[2 * h]]
        bwd = [st_ref[2 * h + 1]]
        if nc > 1:
            a_f = jnp.exp(jnp.full((HEAD_DIM, HEAD_DIM), ldf_ref[layer, h] * rc, F32))
            a_b = jnp.exp(jnp.full((HEAD_DIM, HEAD_DIM), ldb_ref[layer, h] * rc, F32))
            for c in range(nc - 1):
                kc = k_ref[rows[c], cols].astype(F32)
                kv = lax.dot_general((kc * k_dec_f).astype(BF16), v_ref[rows[c], cols], tn, preferred_element_type=F32)
                fwd.append(a_f * fwd[-1] + kv)
            for c in range(nc - 1, 0, -1):
                kc = k_ref[rows[c], cols].astype(F32)
                kv = lax.dot_general((kc * k_dec_b).astype(BF16), v_ref[rows[c], cols], tn, preferred_element_type=F32)
                bwd.append(a_b * bwd[-1] + kv)
            bwd = bwd[::-1]
        for c in range(nc):
            qc = q_ref[rows[c], cols]
            kc = k_ref[rows[c], cols]
            vc = v_ref[rows[c], cols]
            scores = lax.dot_general(qc, kc, nt_dims, preferred_element_type=F32)
            intra = _dot((scores * mask_ref[h]).astype(BF16), vc)
            qf = qc.astype(F32)
            q_both = jnp.concatenate([(qf * q_dec_f).astype(BF16), (qf * q_dec_b).astype(BF16)], axis=1)
            s_both = jnp.concatenate([fwd[c], bwd[c]], axis=0).astype(BF16)
            y = intra + _dot(q_both, s_both)
            yc = y - jnp.mean(y, axis=-1, keepdims=True)
            yn = yc * lax.rsqrt(jnp.mean(yc * yc, axis=-1, keepdims=True) + GN_EPS)
            o_ref[rows[c], cols] = (jax.nn.silu(gs_ref[rows[c], cols]) * yn).astype(BF16)


def _retention(q, k, v, gs, st, mask, dec, ld_f, ld_b, layer):
    b, s, _ = q.shape
    tm = TOKEN_TILE
    rc = mask.shape[-1]
    tok = pl.BlockSpec((None, tm, RET_WIDTH), lambda i, j: (i, j, 0))
    st_spec = pl.BlockSpec((None, None, 2 * RET_HEADS, HEAD_DIM, HEAD_DIM), lambda i, j: (i, j, 0, 0, 0))
    return pl.pallas_call(
        functools.partial(_retention_kernel, tm=tm, rc=rc, layer=layer),
        out_shape=jax.ShapeDtypeStruct((b, s, RET_WIDTH), BF16),
        grid=(b, s // tm),
        in_specs=[_smem(), _smem(), tok, tok, tok, tok, st_spec, _layer_resident(mask.shape, layer),
                  _layer_resident(dec.shape, layer)],
        out_specs=tok,
        compiler_params=_params(2),
        name="retention",
    )(ld_f, ld_b, q, k, v, gs, st, mask, dec)


def _dft_rows_kernel(*refs, n1):
    f_refs, g_ref, y_ref, flat_ref = refs[:FFT_GROUPS], refs[FFT_GROUPS], refs[-2], refs[-1]
    sub = V7X_SUBLANES_F32
    for grp, f_ref in enumerate(f_refs):
        f = f_ref[...].astype(F32)
        for half in range(DFT_ROWS // sub):
            flat_ref[grp, half] = f[:, half * sub:(half + 1) * sub, :].reshape(n1 * sub, FFT_GROUP_DIM)
    for t in range(DFT_ROWS):
        pick = pl.ds(t % sub, n1, stride=sub)
        x = jnp.concatenate([flat_ref[grp, t // sub, pick, :] for grp in range(FFT_GROUPS)], axis=1).astype(BF16)
        y_ref[t] = _dot(g_ref[t], x).astype(BF16)


def _dft_rows(f, g):
    b, n1, n2, width = f.shape
    grp = [pl.BlockSpec((None, n1, DFT_ROWS, FFT_GROUP_DIM), functools.partial(lambda i, j, c: (i, 0, j, c), c=c))
           for c in range(FFT_GROUPS)]
    return pl.pallas_call(
        functools.partial(_dft_rows_kernel, n1=n1),
        out_shape=jax.ShapeDtypeStruct((b, n2, 2 * n1, width), BF16),
        grid=(b, n2 // DFT_ROWS),
        in_specs=grp + [pl.BlockSpec((DFT_ROWS, 2 * n1, n1), lambda i, j: (j, 0, 0))],
        out_specs=pl.BlockSpec((None, DFT_ROWS, 2 * n1, width), lambda i, j: (i, j, 0, 0)),
        scratch_shapes=[pltpu.VMEM((FFT_GROUPS, DFT_ROWS // V7X_SUBLANES_F32, n1 * V7X_SUBLANES_F32, FFT_GROUP_DIM),
                                   F32)],
        compiler_params=_params(2),
        name="dft_rows",
    )(*([f] * FFT_GROUPS), g)


def _dft_cols_kernel(y_ref, m_ref, c_ref, o_ref, yflat_ref, oflat_ref, *, n2):
    sub = V7X_SUBLANES_F32
    halves = DFT_ROWS // sub
    group_cols = [slice(grp * FFT_GROUP_DIM, (grp + 1) * FFT_GROUP_DIM) for grp in range(FFT_GROUPS)]
    for part in range(2):
        for grp in range(FFT_GROUPS):
            y = y_ref[:, part, :, group_cols[grp]].astype(F32)
            for half in range(halves):
                yflat_ref[part, grp, half] = y[:, half * sub:(half + 1) * sub, :].reshape(n2 * sub, FFT_GROUP_DIM)
    for t in range(DFT_ROWS):
        half, pick = t // sub, pl.ds(t % sub, n2, stride=sub)
        a = jnp.concatenate(
            [jnp.concatenate([yflat_ref[part, grp, half, pick, :] for grp in range(FFT_GROUPS)], axis=1)
             for part in range(2)], axis=0).astype(BF16)
        y = _dot(m_ref[...], a)
        y_both = jnp.concatenate([jnp.concatenate([y[:n2, cols], y[n2:, cols]], axis=1) for cols in group_cols],
                                 axis=0).astype(BF16)
        fr = _dot(y_both, c_ref[...])
        for grp in range(FFT_GROUPS):
            oflat_ref[grp, half, pick, :] = fr[grp * n2:(grp + 1) * n2]
    o_ref[...] = jnp.concatenate(
        [jnp.concatenate([oflat_ref[grp, half].reshape(n2, sub, FFT_GROUP_DIM) for half in range(halves)], axis=1)
         for grp in range(FFT_GROUPS)], axis=2).astype(BF16)


def _dft_cols(y, m, c):
    b, n2, two_n1, width = y.shape
    n1 = two_n1 // 2
    y = y.reshape(b, n2, 2, n1, width)
    flat_shape = (FFT_GROUPS, DFT_ROWS // V7X_SUBLANES_F32, n2 * V7X_SUBLANES_F32, FFT_GROUP_DIM)
    return pl.pallas_call(
        functools.partial(_dft_cols_kernel, n2=n2),
        out_shape=jax.ShapeDtypeStruct((b, n2, n1, width), BF16),
        grid=(b, n1 // DFT_ROWS),
        in_specs=[pl.BlockSpec((None, n2, 2, DFT_ROWS, width), lambda i, j: (i, 0, 0, j, 0)),
                  _resident(m.shape), _resident(c.shape)],
        out_specs=pl.BlockSpec((None, n2, DFT_ROWS, width), lambda i, j: (i, 0, j, 0)),
        scratch_shapes=[pltpu.VMEM((2,) + flat_shape, F32), pltpu.VMEM(flat_shape, F32)],
        compiler_params=_params(2),
        name="dft_cols",
    )(y, m, c)


def _dft_tables(s):
    n2 = DFT_N2
    n1 = s // n2
    two_pi = 2.0 * jnp.pi
    scale = (FFT_GROUP_DIM * s) ** -0.5
    i1 = jnp.arange(n1, dtype=jnp.int32)
    ang1 = ((i1[:, None] * i1[None, :]) % n1).astype(F32) * (two_pi / n1)
    angt = (i1[None, :] * jnp.arange(n2, dtype=jnp.int32)[:, None]).astype(F32) * (two_pi / s)
    wr, wi = jnp.cos(ang1)[None], -jnp.sin(ang1)[None]
    tr, ti = (jnp.cos(angt) * scale)[:, :, None], (-jnp.sin(angt) * scale)[:, :, None]
    g = jnp.concatenate([wr * tr - wi * ti, wr * ti + wi * tr], axis=1)
    idx = jnp.arange(n2, dtype=jnp.int32)
    ang2 = ((idx[:, None] * idx[None, :]) % n2).astype(F32) * (two_pi / n2)
    c2, s2 = jnp.cos(ang2), jnp.sin(ang2)
    m = jnp.concatenate([jnp.concatenate([c2, s2], axis=1), jnp.concatenate([-s2, c2], axis=1)], axis=0)
    idc = jnp.arange(FFT_GROUP_DIM, dtype=jnp.int32)
    angc = ((idc[:, None] * idc[None, :]) % FFT_GROUP_DIM).astype(F32) * (two_pi / FFT_GROUP_DIM)
    c = jnp.concatenate([jnp.cos(angc), jnp.sin(angc)], axis=0)
    return g.astype(BF16), m.astype(BF16), c.astype(BF16)


def _mixer_out_kernel(x_ref, g_ref, ret_ref, fr_ref, wg_ref, wr_ref, wf_ref, wm_ref, o_ref, *, d):
    x = x_ref[...]
    xg = (x * g_ref[...]).astype(BF16)
    scale = _rms_scale(x)
    ret = _dot(ret_ref[...], wr_ref[...])
    merged = jax.nn.sigmoid(_dot(xg, wg_ref[:, 0:d]) * scale) * ret
    fou = _dot(fr_ref[...], wf_ref[...])
    merged = merged + jax.nn.sigmoid(_dot(xg, wg_ref[:, d:2 * d]) * scale) * fou
    o_ref[...] = x + _dot(merged.astype(BF16), wm_ref[...])


def _mixer_out(x, g, ret_in, fr, w_gates, w_ret, w_fft, w_mix, layer):
    n_tok, d = x.shape
    tm = TOKEN_TILE
    tok = lambda width: pl.BlockSpec((tm, width), lambda i: (i, 0))
    return pl.pallas_call(
        functools.partial(_mixer_out_kernel, d=d),
        out_shape=jax.ShapeDtypeStruct((n_tok, d), F32),
        grid=(n_tok // tm,),
        in_specs=[tok(d), _layer_resident(g.shape, layer), tok(RET_WIDTH), tok(FFT_WIDTH),
                  _layer_resident(w_gates.shape, layer), _layer_resident(w_ret.shape, layer),
                  _layer_resident(w_fft.shape, layer), _layer_resident(w_mix.shape, layer)],
        out_specs=tok(d),
        compiler_params=_params(1),
        name="mixer_out",
    )(x, g, ret_in, fr, w_gates, w_ret, w_fft, w_mix)


def _rotary_tables(s):
    half = HEAD_DIM // 2
    inv_freq = 1.0 / (ROPE_BASE ** jnp.linspace(0.0, 1.0, half, dtype=F32))
    freq = jnp.concatenate([inv_freq, inv_freq])[None, :]
    sign = jnp.concatenate([-jnp.ones((half,), F32), jnp.ones((half,), F32)])[None, :]
    starts = jnp.arange(0, s, TOKEN_TILE, dtype=F32)[:, None] * freq
    base = jnp.stack([jnp.cos(starts), jnp.sin(starts)], axis=1)[:, :, None, :]
    base = jnp.broadcast_to(base, (base.shape[0], 2, V7X_SUBLANES_F32, HEAD_DIM))
    rows = jnp.arange(TOKEN_TILE, dtype=F32)[:, None] * freq
    cos_r, sin_r = jnp.cos(rows), jnp.sin(rows)
    return base, jnp.stack([cos_r, sin_r, cos_r * sign, sin_r * sign])


def _prepare_weights(g_ffn1, w_ffn1_in, w_ffn1_out, g_mix, w_in, w_ret_out, w_fft_out, w_mix_out,
                     g_ffn2, w_ffn2_in, w_ffn2_out, g_final):
    depth, d, _ = w_in.shape
    n_proj = 4 * RET_WIDTH + FFT_WIDTH
    w_qk = w_in[:, :, :2 * RET_WIDTH].astype(BF16).reshape(depth, d, 2 * RET_HEADS, HEAD_DIM // 2, 2)
    w_qk = jnp.swapaxes(w_qk, 3, 4).reshape(depth, d, 2 * RET_WIDTH)
    w_proj = jnp.concatenate([w_qk, w_in[:, :, 2 * RET_WIDTH:n_proj].astype(BF16)], axis=2)
    gain = lambda g: g[:, None, :]
    return dict(
        g_ffn1=gain(g_ffn1), w_ffn1_in=w_ffn1_in.astype(BF16), w_ffn1_out=w_ffn1_out.astype(BF16),
        g_mix=gain(g_mix), w_proj=w_proj, w_gates=w_in[:, :, n_proj:].astype(BF16),
        w_ret_out=w_ret_out.astype(BF16), w_fft_out=w_fft_out.astype(BF16), w_mix_out=w_mix_out.astype(BF16),
        g_ffn2=gain(g_ffn2), w_ffn2_in=w_ffn2_in.astype(BF16), w_ffn2_out=w_ffn2_out.astype(BF16),
        g_final=g_final[None, None, :],
    )


def _trunk(x, w, ld_f, ld_b, mask, dec):
    b, s, d = x.shape
    assert s % TOKEN_TILE == 0 and TOKEN_TILE % RET_CHUNK == 0 and s % (DFT_N2 * DFT_ROWS) == 0
    n1 = s // DFT_N2
    depth = ld_f.shape[0]
    rot_base, rot_rows = _rotary_tables(s)
    dft_g, dft_m, dft_c = _dft_tables(s)
    x = x.reshape(b * s, d)
    for l in range(depth):
        x = _ffn(x, w["g_ffn1"], w["w_ffn1_in"], w["w_ffn1_out"], w["g_final"], l, final_norm=False)
        q, k, v, gs, f, kv = _mixer_in(x.reshape(b, s, d), w["g_mix"], w["w_proj"], rot_base, rot_rows,
                                       ld_f, ld_b, l)
        st = _ret_states(kv, ld_f, ld_b, l)
        ret_in = _retention(q, k, v, gs, st, mask, dec, ld_f, ld_b, l)
        fr = _dft_cols(_dft_rows(f.reshape(b, n1, DFT_N2, FFT_WIDTH), dft_g), dft_m, dft_c)
        x = _mixer_out(x, w["g_mix"], ret_in.reshape(b * s, RET_WIDTH), fr.reshape(b * s, FFT_WIDTH),
                       w["w_gates"], w["w_ret_out"], w["w_fft_out"], w["w_mix_out"], l)
        x = _ffn(x, w["g_ffn2"], w["w_ffn2_in"], w["w_ffn2_out"], w["g_final"], l, final_norm=(l == depth - 1))
    return x.reshape(b, s, d)


def kernel(x_prompt, x_sample, g_ffn1, w_ffn1_in, w_ffn1_out, g_mix, w_in, ret_log_decay_fwd, ret_log_decay_bwd,
           w_ret_out, w_fft_out, w_mix_out, g_ffn2, w_ffn2_in, w_ffn2_out, g_final):
    w = _prepare_weights(g_ffn1, w_ffn1_in, w_ffn1_out, g_mix, w_in, w_ret_out, w_fft_out, w_mix_out,
                         g_ffn2, w_ffn2_in, w_ffn2_out, g_final)
    mask, dec = _ret_decay(ret_log_decay_fwd, ret_log_decay_bwd, RET_CHUNK)
    return tuple(_trunk(x, w, ret_log_decay_fwd, ret_log_decay_bwd, mask, dec) for x in (x_prompt, x_sample))
```

```python
import functools

import jax
import jax.numpy as jnp
from jax import lax
from jax.experimental import pallas as pl
from jax.experimental.pallas import tpu as pltpu

F32 = jnp.float32
BF16 = jnp.bfloat16

RET_HEADS = 4
HEAD_DIM = 128
RET_WIDTH = RET_HEADS * HEAD_DIM
FFT_GROUPS = 4
FFT_GROUP_DIM = 128
FFT_WIDTH = FFT_GROUPS * FFT_GROUP_DIM
RMS_EPS = 1e-6
GN_EPS = 1e-6
ROPE_BASE = 10000.0

V7X_LANES = 128
V7X_SUBLANES_F32 = 8
V7X_SUBLANES_BF16 = 16
V7X_MXU_COLUMNS = 256
V7X_VMEM_BYTES = 64 * 1024 * 1024
VMEM_LIMIT_BYTES = V7X_VMEM_BYTES - 8 * 1024 * 1024

TOKEN_TILE = 1024
RET_CHUNK = 256
FF_CHUNK = V7X_MXU_COLUMNS
MIX_CHUNK = V7X_MXU_COLUMNS
DFT_N2 = 128
DFT_ROWS = V7X_SUBLANES_BF16


def _params(n_grid_dims):
    return pltpu.CompilerParams(
        dimension_semantics=("parallel",) * n_grid_dims,
        vmem_limit_bytes=VMEM_LIMIT_BYTES,
    )


def _resident(shape):
    return pl.BlockSpec(shape, lambda *_: (0,) * len(shape), pipeline_mode=pl.Buffered(1))


def _layer_resident(shape, layer):
    index = (layer,) + (0,) * (len(shape) - 1)
    return pl.BlockSpec((None,) + tuple(shape[1:]), lambda *_: index, pipeline_mode=pl.Buffered(1))


def _smem():
    return pl.BlockSpec(memory_space=pltpu.SMEM)


def _rms_scale(x):
    return lax.rsqrt(jnp.mean(x * x, axis=-1, keepdims=True) + RMS_EPS)


def _rmsnorm(x, g):
    return x * _rms_scale(x) * g


def _dot(a, b):
    return jnp.dot(a, b, preferred_element_type=F32)


def _ffn_kernel(x_ref, g_ref, win_ref, wout_ref, gfin_ref, o_ref, h_ref, *, d_ff, final_norm):
    x = x_ref[...]
    xg = (x * g_ref[...]).astype(BF16)
    scale = _rms_scale(x)
    for lo in range(0, d_ff, FF_CHUNK):
        a = _dot(xg, win_ref[:, lo:lo + FF_CHUNK]) * scale
        b = _dot(xg, win_ref[:, d_ff + lo:d_ff + lo + FF_CHUNK]) * scale
        h_ref[:, lo:lo + FF_CHUNK] = (jax.nn.silu(a) * b).astype(BF16)
    out = x + 0.5 * _dot(h_ref[...], wout_ref[...])
    if final_norm:
        out = _rmsnorm(out, gfin_ref[...])
    o_ref[...] = out


def _ffn(x, g, w_in, w_out, g_final, layer, *, final_norm):
    n_tok, d = x.shape
    d_ff = w_out.shape[1]
    tm = TOKEN_TILE
    assert n_tok % tm == 0 and d_ff % FF_CHUNK == 0
    tile = pl.BlockSpec((tm, d), lambda i: (i, 0))
    return pl.pallas_call(
        functools.partial(_ffn_kernel, d_ff=d_ff, final_norm=final_norm),
        out_shape=jax.ShapeDtypeStruct((n_tok, d), F32),
        grid=(n_tok // tm,),
        in_specs=[tile, _layer_resident(g.shape, layer), _layer_resident(w_in.shape, layer),
                  _layer_resident(w_out.shape, layer), _layer_resident(g_final.shape, 0)],
        out_specs=tile,
        scratch_shapes=[pltpu.VMEM((tm, d_ff), BF16)],
        compiler_params=_params(1),
        name="ffn",
    )(x, g, w_in, w_out, g_final)


def _mixer_in_kernel(ldf_ref, ldb_ref, x_ref, g_ref, w_ref, base_ref, rot_ref,
                     q_ref, k_ref, v_ref, gs_ref, f_ref, kv_ref, *, tm, layer):
    x = x_ref[...]
    xg = (x * g_ref[...]).astype(BF16)
    scale = _rms_scale(x)
    cos_b, sin_b = base_ref[0, 0:1, :], base_ref[1, 0:1, :]
    cos = cos_b * rot_ref[0] - sin_b * rot_ref[1]
    sin = sin_b * rot_ref[2] + cos_b * rot_ref[3]
    q = _dot(xg, w_ref[:, 0:RET_WIDTH]) * scale
    k = _dot(xg, w_ref[:, RET_WIDTH:2 * RET_WIDTH]) * scale
    v = (_dot(xg, w_ref[:, 2 * RET_WIDTH:3 * RET_WIDTH]) * scale).astype(BF16)
    v_ref[...] = v
    gs_ref[...] = _dot(xg, w_ref[:, 3 * RET_WIDTH:4 * RET_WIDTH]) * scale
    f_ref[...] = (_dot(xg, w_ref[:, 4 * RET_WIDTH:4 * RET_WIDTH + FFT_WIDTH]) * scale).astype(BF16)
    row = lax.broadcasted_iota(jnp.int32, (tm, HEAD_DIM), 0).astype(F32)
    k_scale = HEAD_DIM ** -0.5
    for h in range(RET_HEADS):
        cols = slice(h * HEAD_DIM, (h + 1) * HEAD_DIM)
        qh = q[:, cols]
        q_ref[:, cols] = (qh * cos + pltpu.roll(qh, HEAD_DIM // 2, axis=1) * sin).astype(BF16)
        kh = k[:, cols]
        kr = (kh * cos + pltpu.roll(kh, HEAD_DIM // 2, axis=1) * sin) * k_scale
        k_ref[:, cols] = kr.astype(BF16)
        vh = v[:, cols]
        kf = (kr * jnp.exp(ldf_ref[layer, h] * (tm - 1.0 - row))).astype(BF16)
        kb = (kr * jnp.exp(ldb_ref[layer, h] * row)).astype(BF16)
        tn = (((0,), (0,)), ((), ()))
        kv_ref[2 * h] = lax.dot_general(kf, vh, tn, preferred_element_type=F32)
        kv_ref[2 * h + 1] = lax.dot_general(kb, vh, tn, preferred_element_type=F32)


def _mixer_in(x, g, w, rot_base, rot_rows, ld_f, ld_b, layer):
    b, s, d = x.shape
    tm = TOKEN_TILE
    nt = s // tm
    tok = lambda width: pl.BlockSpec((None, tm, width), lambda i, j: (i, j, 0))
    base = pl.BlockSpec((None,) + rot_base.shape[1:], lambda i, j: (j, 0, 0, 0))
    kv_spec = pl.BlockSpec((None, None, 2 * RET_HEADS, HEAD_DIM, HEAD_DIM), lambda i, j: (i, j, 0, 0, 0))
    act = lambda dtype: jax.ShapeDtypeStruct((b, s, RET_WIDTH), dtype)
    return pl.pallas_call(
        functools.partial(_mixer_in_kernel, tm=tm, layer=layer),
        out_shape=(act(BF16), act(BF16), act(BF16), act(F32), act(BF16),
                   jax.ShapeDtypeStruct((b, nt, 2 * RET_HEADS, HEAD_DIM, HEAD_DIM), F32)),
        grid=(b, nt),
        in_specs=[_smem(), _smem(), tok(d), _layer_resident(g.shape, layer), _layer_resident(w.shape, layer),
                  base, _resident(rot_rows.shape)],
        out_specs=(tok(RET_WIDTH), tok(RET_WIDTH), tok(RET_WIDTH), tok(RET_WIDTH), tok(FFT_WIDTH), kv_spec),
        compiler_params=_params(2),
        name="mixer_in",
    )(ld_f, ld_b, x, g, w, rot_base, rot_rows)


def _ret_states_kernel(ldf_ref, ldb_ref, kv_ref, st_ref, *, nt, tm, layer):
    for h in range(RET_HEADS):
        a_f = jnp.exp(jnp.full((HEAD_DIM, HEAD_DIM), ldf_ref[layer, h] * tm, F32))
        a_b = jnp.exp(jnp.full((HEAD_DIM, HEAD_DIM), ldb_ref[layer, h] * tm, F32))
        state = jnp.zeros((HEAD_DIM, HEAD_DIM), F32)
        for t in range(nt):
            st_ref[t, 2 * h] = state
            state = a_f * state + kv_ref[t, 2 * h]
        state = jnp.zeros((HEAD_DIM, HEAD_DIM), F32)
        for t in reversed(range(nt)):
            st_ref[t, 2 * h + 1] = state
            state = a_b * state + kv_ref[t, 2 * h + 1]


def _ret_states(kv, ld_f, ld_b, layer):
    b, nt = kv.shape[:2]
    spec = pl.BlockSpec((None,) + kv.shape[1:], lambda i: (i, 0, 0, 0, 0))
    return pl.pallas_call(
        functools.partial(_ret_states_kernel, nt=nt, tm=TOKEN_TILE, layer=layer),
        out_shape=jax.ShapeDtypeStruct(kv.shape, F32),
        grid=(b,),
        in_specs=[_smem(), _smem(), spec],
        out_specs=spec,
        compiler_params=_params(1),
        name="ret_states",
    )(ld_f, ld_b, kv)


def _ret_decay_kernel(ldf_ref, ldb_ref, mask_ref, dec_ref, *, rc):
    layer = pl.program_id(0)
    i = lax.broadcasted_iota(jnp.int32, (rc, rc), 0)
    j = lax.broadcasted_iota(jnp.int32, (rc, rc), 1)
    dist = (i - j).astype(F32)
    row = lax.broadcasted_iota(jnp.int32, (rc, HEAD_DIM), 0).astype(F32)
    for h in range(RET_HEADS):
        ld_f = ldf_ref[layer, h]
        ld_b = ldb_ref[layer, h]
        mask_ref[h] = jnp.exp(jnp.where(i >= j, ld_f * dist, -ld_b * dist))
        dec_ref[h, 0] = jnp.exp(ld_f * (row + 1.0)).astype(BF16)
        dec_ref[h, 1] = jnp.exp(ld_b * (rc - row)).astype(BF16)
        dec_ref[h, 2] = jnp.exp(ld_f * (rc - 1.0 - row)).astype(BF16)
        dec_ref[h, 3] = jnp.exp(ld_b * row).astype(BF16)


def _ret_decay(ld_f, ld_b, rc):
    depth = ld_f.shape[0]
    return pl.pallas_call(
        functools.partial(_ret_decay_kernel, rc=rc),
        out_shape=(jax.ShapeDtypeStruct((depth, RET_HEADS, rc, rc), F32),
                   jax.ShapeDtypeStruct((depth, RET_HEADS, 4, rc, HEAD_DIM), BF16)),
        grid=(depth,),
        in_specs=[_smem(), _smem()],
        out_specs=(pl.BlockSpec((None, RET_HEADS, rc, rc), lambda l: (l, 0, 0, 0)),
                   pl.BlockSpec((None, RET_HEADS, 4, rc, HEAD_DIM), lambda l: (l, 0, 0, 0, 0))),
        compiler_params=_params(1),
        name="ret_decay",
    )(ld_f, ld_b)


def _retention_kernel(ldf_ref, ldb_ref, q_ref, k_ref, v_ref, gs_ref, st_ref, mask_ref, dec_ref, o_ref,
                      *, tm, rc, layer):
    nc = tm // rc
    tn = (((0,), (0,)), ((), ()))
    nt_dims = (((1,), (1,)), ((), ()))
    for h in range(RET_HEADS):
        cols = slice(h * HEAD_DIM, (h + 1) * HEAD_DIM)
        q_dec_f, q_dec_b, k_dec_f, k_dec_b = (dec_ref[h, n] for n in range(4))
        rows = [slice(c * rc, (c + 1) * rc) for c in range(nc)]
        fwd = [st_ref[2 * h]]
        bwd = [st_ref[2 * h + 1]]
        if nc > 1:
            a_f = jnp.exp(jnp.full((HEAD_DIM, HEAD_DIM), ldf_ref[layer, h] * rc, F32))
            a_b = jnp.exp(jnp.full((HEAD_DIM, HEAD_DIM), ldb_ref[layer, h] * rc, F32))
            for c in range(nc - 1):
                kv = lax.dot_general(k_ref[rows[c], cols] * k_dec_f, v_ref[rows[c], cols], tn,
                                     preferred_element_type=F32)
                fwd.append(a_f * fwd[-1] + kv)
            for c in range(nc - 1, 0, -1):
                kv = lax.dot_general(k_ref[rows[c], cols] * k_dec_b, v_ref[rows[c], cols], tn,
                                     preferred_element_type=F32)
                bwd.append(a_b * bwd[-1] + kv)
            bwd = bwd[::-1]
        for c in range(nc):
            qc = q_ref[rows[c], cols]
            kc = k_ref[rows[c], cols]
            vc = v_ref[rows[c], cols]
            scores = lax.dot_general(qc, kc, nt_dims, preferred_element_type=F32)
            intra = _dot((scores * mask_ref[h]).astype(BF16), vc)
            q_both = jnp.concatenate([qc * q_dec_f, qc * q_dec_b], axis=1)
            s_both = jnp.concatenate([fwd[c], bwd[c]], axis=0).astype(BF16)
            y = intra + _dot(q_both, s_both)
            yc = y - jnp.mean(y, axis=-1, keepdims=True)
            yn = yc * lax.rsqrt(jnp.mean(yc * yc, axis=-1, keepdims=True) + GN_EPS)
            o_ref[rows[c], cols] = (jax.nn.silu(gs_ref[rows[c], cols]) * yn).astype(BF16)


def _retention(q, k, v, gs, st, mask, dec, ld_f, ld_b, layer):
    b, s, _ = q.shape
    tm = TOKEN_TILE
    rc = mask.shape[-1]
    tok = pl.BlockSpec((None, tm, RET_WIDTH), lambda i, j: (i, j, 0))
    st_spec = pl.BlockSpec((None, None, 2 * RET_HEADS, HEAD_DIM, HEAD_DIM), lambda i, j: (i, j, 0, 0, 0))
    return pl.pallas_call(
        functools.partial(_retention_kernel, tm=tm, rc=rc, layer=layer),
        out_shape=jax.ShapeDtypeStruct((b, s, RET_WIDTH), BF16),
        grid=(b, s // tm),
        in_specs=[_smem(), _smem(), tok, tok, tok, tok, st_spec, _layer_resident(mask.shape, layer),
                  _layer_resident(dec.shape, layer)],
        out_specs=tok,
        compiler_params=_params(2),
        name="retention",
    )(ld_f, ld_b, q, k, v, gs, st, mask, dec)


def _dft_rows_kernel(*refs, n1):
    f_refs, g_ref, y_ref, flat_ref = refs[:FFT_GROUPS], refs[FFT_GROUPS], refs[-2], refs[-1]
    sub = V7X_SUBLANES_F32
    for grp, f_ref in enumerate(f_refs):
        f = f_ref[...].astype(F32)
        for half in range(DFT_ROWS // sub):
            flat_ref[grp, half] = f[:, half * sub:(half + 1) * sub, :].reshape(n1 * sub, FFT_GROUP_DIM)
    for t in range(DFT_ROWS):
        pick = pl.ds(t % sub, n1, stride=sub)
        x = jnp.concatenate([flat_ref[grp, t // sub, pick, :] for grp in range(FFT_GROUPS)], axis=1).astype(BF16)
        y_ref[t] = _dot(g_ref[t], x).astype(BF16)


def _dft_rows(f, g):
    b, n1, n2, width = f.shape
    grp = [pl.BlockSpec((None, n1, DFT_ROWS, FFT_GROUP_DIM), functools.partial(lambda i, j, c: (i, 0, j, c), c=c))
           for c in range(FFT_GROUPS)]
    return pl.pallas_call(
        functools.partial(_dft_rows_kernel, n1=n1),
        out_shape=jax.ShapeDtypeStruct((b, n2, 2 * n1, width), BF16),
        grid=(b, n2 // DFT_ROWS),
        in_specs=grp + [pl.BlockSpec((DFT_ROWS, 2 * n1, n1), lambda i, j: (j, 0, 0))],
        out_specs=pl.BlockSpec((None, DFT_ROWS, 2 * n1, width), lambda i, j: (i, j, 0, 0)),
        scratch_shapes=[pltpu.VMEM((FFT_GROUPS, DFT_ROWS // V7X_SUBLANES_F32, n1 * V7X_SUBLANES_F32, FFT_GROUP_DIM),
                                   F32)],
        compiler_params=_params(2),
        name="dft_rows",
    )(*([f] * FFT_GROUPS), g)


def _dft_cols_kernel(y_ref, m_ref, c_ref, o_ref, yflat_ref, oflat_ref, *, n2):
    sub = V7X_SUBLANES_F32
    halves = DFT_ROWS // sub
    group_cols = [slice(grp * FFT_GROUP_DIM, (grp + 1) * FFT_GROUP_DIM) for grp in range(FFT_GROUPS)]
    for part in range(2):
        for grp in range(FFT_GROUPS):
            y = y_ref[:, part, :, group_cols[grp]].astype(F32)
            for half in range(halves):
                yflat_ref[part, grp, half] = y[:, half * sub:(half + 1) * sub, :].reshape(n2 * sub, FFT_GROUP_DIM)
    for t in range(DFT_ROWS):
        half, pick = t // sub, pl.ds(t % sub, n2, stride=sub)
        a = jnp.concatenate(
            [jnp.concatenate([yflat_ref[part, grp, half, pick, :] for grp in range(FFT_GROUPS)], axis=1)
             for part in range(2)], axis=0).astype(BF16)
        y = _dot(m_ref[...], a)
        y_both = jnp.concatenate([jnp.concatenate([y[:n2, cols], y[n2:, cols]], axis=1) for cols in group_cols],
                                 axis=0).astype(BF16)
        fr = _dot(y_both, c_ref[...])
        for grp in range(FFT_GROUPS):
            oflat_ref[grp, half, pick, :] = fr[grp * n2:(grp + 1) * n2]
    o_ref[...] = jnp.concatenate(
        [jnp.concatenate([oflat_ref[grp, half].reshape(n2, sub, FFT_GROUP_DIM) for half in range(halves)], axis=1)
         for grp in range(FFT_GROUPS)], axis=2).astype(BF16)


def _dft_cols(y, m, c):
    b, n2, two_n1, width = y.shape
    n1 = two_n1 // 2
    y = y.reshape(b, n2, 2, n1, width)
    flat_shape = (FFT_GROUPS, DFT_ROWS // V7X_SUBLANES_F32, n2 * V7X_SUBLANES_F32, FFT_GROUP_DIM)
    return pl.pallas_call(
        functools.partial(_dft_cols_kernel, n2=n2),
        out_shape=jax.ShapeDtypeStruct((b, n2, n1, width), BF16),
        grid=(b, n1 // DFT_ROWS),
        in_specs=[pl.BlockSpec((None, n2, 2, DFT_ROWS, width), lambda i, j: (i, 0, 0, j, 0)),
                  _resident(m.shape), _resident(c.shape)],
        out_specs=pl.BlockSpec((None, n2, DFT_ROWS, width), lambda i, j: (i, 0, j, 0)),
        scratch_shapes=[pltpu.VMEM((2,) + flat_shape, F32), pltpu.VMEM(flat_shape, F32)],
        compiler_params=_params(2),
        name="dft_cols",
    )(y, m, c)


def _dft_tables(s):
    n2 = DFT_N2
    n1 = s // n2
    two_pi = 2.0 * jnp.pi
    scale = (FFT_GROUP_DIM * s) ** -0.5
    i1 = jnp.arange(n1, dtype=jnp.int32)
    ang1 = ((i1[:, None] * i1[None, :]) % n1).astype(F32) * (two_pi / n1)
    angt = (i1[None, :] * jnp.arange(n2, dtype=jnp.int32)[:, None]).astype(F32) * (two_pi / s)
    wr, wi = jnp.cos(ang1)[None], -jnp.sin(ang1)[None]
    tr, ti = (jnp.cos(angt) * scale)[:, :, None], (-jnp.sin(angt) * scale)[:, :, None]
    g = jnp.concatenate([wr * tr - wi * ti, wr * ti + wi * tr], axis=1)
    idx = jnp.arange(n2, dtype=jnp.int32)
    ang2 = ((idx[:, None] * idx[None, :]) % n2).astype(F32) * (two_pi / n2)
    c2, s2 = jnp.cos(ang2), jnp.sin(ang2)
    m = jnp.concatenate([jnp.concatenate([c2, s2], axis=1), jnp.concatenate([-s2, c2], axis=1)], axis=0)
    idc = jnp.arange(FFT_GROUP_DIM, dtype=jnp.int32)
    angc = ((idc[:, None] * idc[None, :]) % FFT_GROUP_DIM).astype(F32) * (two_pi / FFT_GROUP_DIM)
    c = jnp.concatenate([jnp.cos(angc), jnp.sin(angc)], axis=0)
    return g.astype(BF16), m.astype(BF16), c.astype(BF16)


def _mixer_out_kernel(x_ref, g_ref, ret_ref, fr_ref, wg_ref, wr_ref, wf_ref, wm_ref, o_ref, merged_ref, *, d):
    x = x_ref[...]
    xg = (x * g_ref[...]).astype(BF16)
    scale = _rms_scale(x)
    for lo in range(0, d, MIX_CHUNK):
        cols = slice(lo, lo + MIX_CHUNK)
        gate_ret = jax.nn.sigmoid(_dot(xg, wg_ref[:, lo:lo + MIX_CHUNK]) * scale)
        gate_fft = jax.nn.sigmoid(_dot(xg, wg_ref[:, d + lo:d + lo + MIX_CHUNK]) * scale)
        merged = gate_ret * _dot(ret_ref[...], wr_ref[:, cols]) + gate_fft * _dot(fr_ref[...], wf_ref[:, cols])
        merged_ref[:, cols] = merged.astype(BF16)
    o_ref[...] = x + _dot(merged_ref[...], wm_ref[...])


def _mixer_out(x, g, ret_in, fr, w_gates, w_ret, w_fft, w_mix, layer):
    n_tok, d = x.shape
    tm = TOKEN_TILE
    tok = lambda width: pl.BlockSpec((tm, width), lambda i: (i, 0))
    return pl.pallas_call(
        functools.partial(_mixer_out_kernel, d=d),
        out_shape=jax.ShapeDtypeStruct((n_tok, d), F32),
        grid=(n_tok // tm,),
        in_specs=[tok(d), _layer_resident(g.shape, layer), tok(RET_WIDTH), tok(FFT_WIDTH),
                  _layer_resident(w_gates.shape, layer), _layer_resident(w_ret.shape, layer),
                  _layer_resident(w_fft.shape, layer), _layer_resident(w_mix.shape, layer)],
        out_specs=tok(d),
        scratch_shapes=[pltpu.VMEM((tm, d), BF16)],
        compiler_params=_params(1),
        name="mixer_out",
    )(x, g, ret_in, fr, w_gates, w_ret, w_fft, w_mix)


def _rotary_tables(s):
    half = HEAD_DIM // 2
    inv_freq = 1.0 / (ROPE_BASE ** jnp.linspace(0.0, 1.0, half, dtype=F32))
    freq = jnp.concatenate([inv_freq, inv_freq])[None, :]
    sign = jnp.concatenate([-jnp.ones((half,), F32), jnp.ones((half,), F32)])[None, :]
    starts = jnp.arange(0, s, TOKEN_TILE, dtype=F32)[:, None] * freq
    base = jnp.stack([jnp.cos(starts), jnp.sin(starts)], axis=1)[:, :, None, :]
    base = jnp.broadcast_to(base, (base.shape[0], 2, V7X_SUBLANES_F32, HEAD_DIM))
    rows = jnp.arange(TOKEN_TILE, dtype=F32)[:, None] * freq
    cos_r, sin_r = jnp.cos(rows), jnp.sin(rows)
    return base, jnp.stack([cos_r, sin_r, cos_r * sign, sin_r * sign])


def _prepare_weights(g_ffn1, w_ffn1_in, w_ffn1_out, g_mix, w_in, w_ret_out, w_fft_out, w_mix_out,
                     g_ffn2, w_ffn2_in, w_ffn2_out, g_final):
    depth, d, _ = w_in.shape
    n_proj = 4 * RET_WIDTH + FFT_WIDTH
    w_qk = w_in[:, :, :2 * RET_WIDTH].astype(BF16).reshape(depth, d, 2 * RET_HEADS, HEAD_DIM // 2, 2)
    w_qk = jnp.swapaxes(w_qk, 3, 4).reshape(depth, d, 2 * RET_WIDTH)
    w_proj = jnp.concatenate([w_qk, w_in[:, :, 2 * RET_WIDTH:n_proj].astype(BF16)], axis=2)
    gain = lambda g: g[:, None, :]
    return dict(
        g_ffn1=gain(g_ffn1), w_ffn1_in=w_ffn1_in.astype(BF16), w_ffn1_out=w_ffn1_out.astype(BF16),
        g_mix=gain(g_mix), w_proj=w_proj, w_gates=w_in[:, :, n_proj:].astype(BF16),
        w_ret_out=w_ret_out.astype(BF16), w_fft_out=w_fft_out.astype(BF16), w_mix_out=w_mix_out.astype(BF16),
        g_ffn2=gain(g_ffn2), w_ffn2_in=w_ffn2_in.astype(BF16), w_ffn2_out=w_ffn2_out.astype(BF16),
        g_final=g_final[None, None, :],
    )


def _trunk(x, w, ld_f, ld_b, mask, dec):
    b, s, d = x.shape
    assert s % TOKEN_TILE == 0 and TOKEN_TILE % RET_CHUNK == 0 and s % (DFT_N2 * DFT_ROWS) == 0
    n1 = s // DFT_N2
    depth = ld_f.shape[0]
    rot_base, rot_rows = _rotary_tables(s)
    dft_g, dft_m, dft_c = _dft_tables(s)
    x = x.reshape(b * s, d)
    for l in range(depth):
        x = _ffn(x, w["g_ffn1"], w["w_ffn1_in"], w["w_ffn1_out"], w["g_final"], l, final_norm=False)
        q, k, v, gs, f, kv = _mixer_in(x.reshape(b, s, d), w["g_mix"], w["w_proj"], rot_base, rot_rows,
                                       ld_f, ld_b, l)
        st = _ret_states(kv, ld_f, ld_b, l)
        ret_in = _retention(q, k, v, gs, st, mask, dec, ld_f, ld_b, l)
        fr = _dft_cols(_dft_rows(f.reshape(b, n1, DFT_N2, FFT_WIDTH), dft_g), dft_m, dft_c)
        x = _mixer_out(x, w["g_mix"], ret_in.reshape(b * s, RET_WIDTH), fr.reshape(b * s, FFT_WIDTH),
                       w["w_gates"], w["w_ret_out"], w["w_fft_out"], w["w_mix_out"], l)
        x = _ffn(x, w["g_ffn2"], w["w_ffn2_in"], w["w_ffn2_out"], w["g_final"], l, final_norm=(l == depth - 1))
    return x.reshape(b, s, d)


def kernel(x_prompt, x_sample, g_ffn1, w_ffn1_in, w_ffn1_out, g_mix, w_in, ret_log_decay_fwd, ret_log_decay_bwd,
           w_ret_out, w_fft_out, w_mix_out, g_ffn2, w_ffn2_in, w_ffn2_out, g_final):
    w = _prepare_weights(g_ffn1, w_ffn1_in, w_ffn1_out, g_mix, w_in, w_ret_out, w_fft_out, w_mix_out,
                         g_ffn2, w_ffn2_in, w_ffn2_out, g_final)
    mask, dec = _ret_decay(ret_log_decay_fwd, ret_log_decay_bwd, RET_CHUNK)
    return tuple(_trunk(x, w, ret_log_decay_fwd, ret_log_decay_bwd, mask, dec) for x in (x_prompt, x_sample))
```

```python
import functools

import jax
import jax.numpy as jnp
from jax import lax
from jax.experimental import pallas as pl
from jax.experimental.pallas import tpu as pltpu

F32 = jnp.float32
BF16 = jnp.bfloat16

RET_HEADS = 4
HEAD_DIM = 128
RET_WIDTH = RET_HEADS * HEAD_DIM
FFT_GROUPS = 4
FFT_GROUP_DIM = 128
FFT_WIDTH = FFT_GROUPS * FFT_GROUP_DIM
RMS_EPS = 1e-6
GN_EPS = 1e-6
ROPE_BASE = 10000.0

V7X_LANES = 128
V7X_SUBLANES_F32 = 8
V7X_SUBLANES_BF16 = 16
V7X_MXU_COLUMNS = 256
V7X_VMEM_BYTES = 64 * 1024 * 1024
VMEM_LIMIT_BYTES = V7X_VMEM_BYTES - 8 * 1024 * 1024

TOKEN_TILE = 1024
RET_CHUNK = 256
FF_CHUNK = V7X_MXU_COLUMNS
MIX_CHUNK = V7X_MXU_COLUMNS
DFT_N2 = 128
DFT_ROWS = V7X_SUBLANES_BF16


def _params(n_grid_dims):
    return pltpu.CompilerParams(
        dimension_semantics=("parallel",) * n_grid_dims,
        vmem_limit_bytes=VMEM_LIMIT_BYTES,
    )


def _resident(shape):
    return pl.BlockSpec(shape, lambda *_: (0,) * len(shape), pipeline_mode=pl.Buffered(1))


def _layer_resident(shape, layer):
    index = (layer,) + (0,) * (len(shape) - 1)
    return pl.BlockSpec((None,) + tuple(shape[1:]), lambda *_: index, pipeline_mode=pl.Buffered(1))


def _smem():
    return pl.BlockSpec(memory_space=pltpu.SMEM)


def _rms_scale(x):
    return lax.rsqrt(jnp.mean(x * x, axis=-1, keepdims=True) + RMS_EPS)


def _rmsnorm(x, g):
    return x * _rms_scale(x) * g


def _dot(a, b):
    return jnp.dot(a, b, preferred_element_type=F32)


def _ffn_kernel(x_ref, g_ref, win_ref, wout_ref, gfin_ref, o_ref, h_ref, *, d_ff, final_norm):
    x = x_ref[...]
    xg = (x * g_ref[...]).astype(BF16)
    scale = _rms_scale(x)
    for lo in range(0, d_ff, FF_CHUNK):
        a = _dot(xg, win_ref[:, lo:lo + FF_CHUNK]) * scale
        b = _dot(xg, win_ref[:, d_ff + lo:d_ff + lo + FF_CHUNK]) * scale
        h_ref[:, lo:lo + FF_CHUNK] = (jax.nn.silu(a) * b).astype(BF16)
    out = x + 0.5 * _dot(h_ref[...], wout_ref[...])
    if final_norm:
        out = _rmsnorm(out, gfin_ref[...])
    o_ref[...] = out


def _ffn(x, g, w_in, w_out, g_final, layer, *, final_norm):
    n_tok, d = x.shape
    d_ff = w_out.shape[1]
    tm = TOKEN_TILE
    assert n_tok % tm == 0 and d_ff % FF_CHUNK == 0
    tile = pl.BlockSpec((tm, d), lambda i: (i, 0))
    return pl.pallas_call(
        functools.partial(_ffn_kernel, d_ff=d_ff, final_norm=final_norm),
        out_shape=jax.ShapeDtypeStruct((n_tok, d), F32),
        grid=(n_tok // tm,),
        in_specs=[tile, _layer_resident(g.shape, layer), _layer_resident(w_in.shape, layer),
                  _layer_resident(w_out.shape, layer), _layer_resident(g_final.shape, 0)],
        out_specs=tile,
        scratch_shapes=[pltpu.VMEM((tm, d_ff), BF16)],
        compiler_params=_params(1),
        name="ffn",
    )(x, g, w_in, w_out, g_final)


def _mixer_in_kernel(ldf_ref, ldb_ref, x_ref, g_ref, w_ref, base_ref, rot_ref,
                     q_ref, k_ref, v_ref, gs_ref, f_ref, kv_ref, *, tm, layer):
    x = x_ref[...]
    xg = (x * g_ref[...]).astype(BF16)
    scale = _rms_scale(x)
    cos_b, sin_b = base_ref[0, 0:1, :], base_ref[1, 0:1, :]
    cos = cos_b * rot_ref[0] - sin_b * rot_ref[1]
    sin = sin_b * rot_ref[2] + cos_b * rot_ref[3]
    q = _dot(xg, w_ref[:, 0:RET_WIDTH]) * scale
    k = _dot(xg, w_ref[:, RET_WIDTH:2 * RET_WIDTH]) * scale
    v = (_dot(xg, w_ref[:, 2 * RET_WIDTH:3 * RET_WIDTH]) * scale).astype(BF16)
    v_ref[...] = v
    gs_ref[...] = (_dot(xg, w_ref[:, 3 * RET_WIDTH:4 * RET_WIDTH]) * scale).astype(BF16)
    f_ref[...] = (_dot(xg, w_ref[:, 4 * RET_WIDTH:4 * RET_WIDTH + FFT_WIDTH]) * scale).astype(BF16)
    row = lax.broadcasted_iota(jnp.int32, (tm, HEAD_DIM), 0).astype(F32)
    k_scale = HEAD_DIM ** -0.5
    for h in range(RET_HEADS):
        cols = slice(h * HEAD_DIM, (h + 1) * HEAD_DIM)
        qh = q[:, cols]
        q_ref[:, cols] = (qh * cos + pltpu.roll(qh, HEAD_DIM // 2, axis=1) * sin).astype(BF16)
        kh = k[:, cols]
        kr = (kh * cos + pltpu.roll(kh, HEAD_DIM // 2, axis=1) * sin) * k_scale
        k_ref[:, cols] = kr.astype(BF16)
        vh = v[:, cols]
        kf = (kr * jnp.exp(ldf_ref[layer, h] * (tm - 1.0 - row))).astype(BF16)
        kb = (kr * jnp.exp(ldb_ref[layer, h] * row)).astype(BF16)
        tn = (((0,), (0,)), ((), ()))
        kv_ref[2 * h] = lax.dot_general(kf, vh, tn, preferred_element_type=F32)
        kv_ref[2 * h + 1] = lax.dot_general(kb, vh, tn, preferred_element_type=F32)


def _mixer_in(x, g, w, rot_base, rot_rows, ld_f, ld_b, layer):
    b, s, d = x.shape
    tm = TOKEN_TILE
    nt = s // tm
    tok = lambda width: pl.BlockSpec((None, tm, width), lambda i, j: (i, j, 0))
    base = pl.BlockSpec((None,) + rot_base.shape[1:], lambda i, j: (j, 0, 0, 0))
    kv_spec = pl.BlockSpec((None, None, 2 * RET_HEADS, HEAD_DIM, HEAD_DIM), lambda i, j: (i, j, 0, 0, 0))
    act = lambda dtype: jax.ShapeDtypeStruct((b, s, RET_WIDTH), dtype)
    return pl.pallas_call(
        functools.partial(_mixer_in_kernel, tm=tm, layer=layer),
        out_shape=(act(BF16), act(BF16), act(BF16), act(BF16), act(BF16),
                   jax.ShapeDtypeStruct((b, nt, 2 * RET_HEADS, HEAD_DIM, HEAD_DIM), F32)),
        grid=(b, nt),
        in_specs=[_smem(), _smem(), tok(d), _layer_resident(g.shape, layer), _layer_resident(w.shape, layer),
                  base, _resident(rot_rows.shape)],
        out_specs=(tok(RET_WIDTH), tok(RET_WIDTH), tok(RET_WIDTH), tok(RET_WIDTH), tok(FFT_WIDTH), kv_spec),
        compiler_params=_params(2),
        name="mixer_in",
    )(ld_f, ld_b, x, g, w, rot_base, rot_rows)


def _ret_states_kernel(ldf_ref, ldb_ref, kv_ref, st_ref, *, nt, tm, layer):
    for h in range(RET_HEADS):
        a_f = jnp.exp(jnp.full((HEAD_DIM, HEAD_DIM), ldf_ref[layer, h] * tm, F32))
        a_b = jnp.exp(jnp.full((HEAD_DIM, HEAD_DIM), ldb_ref[layer, h] * tm, F32))
        state = jnp.zeros((HEAD_DIM, HEAD_DIM), F32)
        for t in range(nt):
            st_ref[t, 2 * h] = state
            state = a_f * state + kv_ref[t, 2 * h]
        state = jnp.zeros((HEAD_DIM, HEAD_DIM), F32)
        for t in reversed(range(nt)):
            st_ref[t, 2 * h + 1] = state
            state = a_b * state + kv_ref[t, 2 * h + 1]


def _ret_states(kv, ld_f, ld_b, layer):
    b, nt = kv.shape[:2]
    spec = pl.BlockSpec((None,) + kv.shape[1:], lambda i: (i, 0, 0, 0, 0))
    return pl.pallas_call(
        functools.partial(_ret_states_kernel, nt=nt, tm=TOKEN_TILE, layer=layer),
        out_shape=jax.ShapeDtypeStruct(kv.shape, F32),
        grid=(b,),
        in_specs=[_smem(), _smem(), spec],
        out_specs=spec,
        compiler_params=_params(1),
        name="ret_states",
    )(ld_f, ld_b, kv)


def _ret_decay_kernel(ldf_ref, ldb_ref, mask_ref, dec_ref, *, rc):
    layer = pl.program_id(0)
    i = lax.broadcasted_iota(jnp.int32, (rc, rc), 0)
    j = lax.broadcasted_iota(jnp.int32, (rc, rc), 1)
    dist = (i - j).astype(F32)
    row = lax.broadcasted_iota(jnp.int32, (rc, HEAD_DIM), 0).astype(F32)
    for h in range(RET_HEADS):
        ld_f = ldf_ref[layer, h]
        ld_b = ldb_ref[layer, h]
        mask_ref[h] = jnp.exp(jnp.where(i >= j, ld_f * dist, -ld_b * dist))
        dec_ref[h, 0] = jnp.exp(ld_f * (row + 1.0)).astype(BF16)
        dec_ref[h, 1] = jnp.exp(ld_b * (rc - row)).astype(BF16)
        dec_ref[h, 2] = jnp.exp(ld_f * (rc - 1.0 - row)).astype(BF16)
        dec_ref[h, 3] = jnp.exp(ld_b * row).astype(BF16)


def _ret_decay(ld_f, ld_b, rc):
    depth = ld_f.shape[0]
    return pl.pallas_call(
        functools.partial(_ret_decay_kernel, rc=rc),
        out_shape=(jax.ShapeDtypeStruct((depth, RET_HEADS, rc, rc), F32),
                   jax.ShapeDtypeStruct((depth, RET_HEADS, 4, rc, HEAD_DIM), BF16)),
        grid=(depth,),
        in_specs=[_smem(), _smem()],
        out_specs=(pl.BlockSpec((None, RET_HEADS, rc, rc), lambda l: (l, 0, 0, 0)),
                   pl.BlockSpec((None, RET_HEADS, 4, rc, HEAD_DIM), lambda l: (l, 0, 0, 0, 0))),
        compiler_params=_params(1),
        name="ret_decay",
    )(ld_f, ld_b)


def _retention_tile(ldf_ref, ldb_ref, q_ref, k_ref, v_ref, gs_ref, st_ref, mask_ref, dec_ref, o_ref,
                    *, tm, rc, layer):
    nc = tm // rc
    tn = (((0,), (0,)), ((), ()))
    nt_dims = (((1,), (1,)), ((), ()))
    for h in range(RET_HEADS):
        cols = slice(h * HEAD_DIM, (h + 1) * HEAD_DIM)
        q_dec_f, q_dec_b, k_dec_f, k_dec_b = (dec_ref[h, n] for n in range(4))
        rows = [slice(c * rc, (c + 1) * rc) for c in range(nc)]
        fwd = [st_ref[2 * h]]
        bwd = [st_ref[2 * h + 1]]
        if nc > 1:
            a_f = jnp.exp(jnp.full((HEAD_DIM, HEAD_DIM), ldf_ref[layer, h] * rc, F32))
            a_b = jnp.exp(jnp.full((HEAD_DIM, HEAD_DIM), ldb_ref[layer, h] * rc, F32))
            for c in range(nc - 1):
                kv = lax.dot_general(k_ref[rows[c], cols] * k_dec_f, v_ref[rows[c], cols], tn,
                                     preferred_element_type=F32)
                fwd.append(a_f * fwd[-1] + kv)
            for c in range(nc - 1, 0, -1):
                kv = lax.dot_general(k_ref[rows[c], cols] * k_dec_b, v_ref[rows[c], cols], tn,
                                     preferred_element_type=F32)
                bwd.append(a_b * bwd[-1] + kv)
            bwd = bwd[::-1]
        for c in range(nc):
            qc = q_ref[rows[c], cols]
            kc = k_ref[rows[c], cols]
            vc = v_ref[rows[c], cols]
            scores = lax.dot_general(qc, kc, nt_dims, preferred_element_type=F32)
            intra = _dot((scores * mask_ref[h]).astype(BF16), vc)
            q_both = jnp.concatenate([qc * q_dec_f, qc * q_dec_b], axis=1)
            s_both = jnp.concatenate([fwd[c], bwd[c]], axis=0).astype(BF16)
            y = intra + _dot(q_both, s_both)
            yc = y - jnp.mean(y, axis=-1, keepdims=True)
            yn = yc * lax.rsqrt(jnp.mean(yc * yc, axis=-1, keepdims=True) + GN_EPS)
            o_ref[rows[c], cols] = (jax.nn.silu(gs_ref[rows[c], cols].astype(F32)) * yn).astype(BF16)


def _dft_rows_kernel(*refs, n1):
    f_refs, g_ref, y_ref, flat_ref = refs[:FFT_GROUPS], refs[FFT_GROUPS], refs[-2], refs[-1]
    sub = V7X_SUBLANES_F32
    for grp, f_ref in enumerate(f_refs):
        f = f_ref[...].astype(F32)
        for half in range(DFT_ROWS // sub):
            flat_ref[grp, half] = f[:, half * sub:(half + 1) * sub, :].reshape(n1 * sub, FFT_GROUP_DIM)
    for t in range(DFT_ROWS):
        pick = pl.ds(t % sub, n1, stride=sub)
        x = jnp.concatenate([flat_ref[grp, t // sub, pick, :] for grp in range(FFT_GROUPS)], axis=1).astype(BF16)
        y_ref[t] = _dot(g_ref[t], x).astype(BF16)


def _dft_rows(f, g):
    b, n1, n2, width = f.shape
    grp = [pl.BlockSpec((None, n1, DFT_ROWS, FFT_GROUP_DIM), functools.partial(lambda i, j, c: (i, 0, j, c), c=c))
           for c in range(FFT_GROUPS)]
    return pl.pallas_call(
        functools.partial(_dft_rows_kernel, n1=n1),
        out_shape=jax.ShapeDtypeStruct((b, n2, 2 * n1, width), BF16),
        grid=(b, n2 // DFT_ROWS),
        in_specs=grp + [pl.BlockSpec((DFT_ROWS, 2 * n1, n1), lambda i, j: (j, 0, 0))],
        out_specs=pl.BlockSpec((None, DFT_ROWS, 2 * n1, width), lambda i, j: (i, j, 0, 0)),
        scratch_shapes=[pltpu.VMEM((FFT_GROUPS, DFT_ROWS // V7X_SUBLANES_F32, n1 * V7X_SUBLANES_F32, FFT_GROUP_DIM),
                                   F32)],
        compiler_params=_params(2),
        name="dft_rows",
    )(*([f] * FFT_GROUPS), g)


def _dft_cols_kernel(y_ref, m_ref, c_ref, o_ref, yflat_ref, oflat_ref, *, n2):
    sub = V7X_SUBLANES_F32
    halves = DFT_ROWS // sub
    group_cols = [slice(grp * FFT_GROUP_DIM, (grp + 1) * FFT_GROUP_DIM) for grp in range(FFT_GROUPS)]
    for part in range(2):
        for grp in range(FFT_GROUPS):
            y = y_ref[:, part, :, group_cols[grp]].astype(F32)
            for half in range(halves):
                yflat_ref[part, grp, half] = y[:, half * sub:(half + 1) * sub, :].reshape(n2 * sub, FFT_GROUP_DIM)
    for t in range(DFT_ROWS):
        half, pick = t // sub, pl.ds(t % sub, n2, stride=sub)
        a = jnp.concatenate(
            [jnp.concatenate([yflat_ref[part, grp, half, pick, :] for grp in range(FFT_GROUPS)], axis=1)
             for part in range(2)], axis=0).astype(BF16)
        y = _dot(m_ref[...], a)
        y_both = jnp.concatenate([jnp.concatenate([y[:n2, cols], y[n2:, cols]], axis=1) for cols in group_cols],
                                 axis=0).astype(BF16)
        fr = _dot(y_both, c_ref[...])
        for grp in range(FFT_GROUPS):
            oflat_ref[grp, half, pick, :] = fr[grp * n2:(grp + 1) * n2]
    o_ref[...] = jnp.concatenate(
        [jnp.concatenate([oflat_ref[grp, half].reshape(n2, sub, FFT_GROUP_DIM) for half in range(halves)], axis=1)
         for grp in range(FFT_GROUPS)], axis=2).astype(BF16)


def _dft_cols(y, m, c):
    b, n2, two_n1, width = y.shape
    n1 = two_n1 // 2
    y = y.reshape(b, n2, 2, n1, width)
    flat_shape = (FFT_GROUPS, DFT_ROWS // V7X_SUBLANES_F32, n2 * V7X_SUBLANES_F32, FFT_GROUP_DIM)
    return pl.pallas_call(
        functools.partial(_dft_cols_kernel, n2=n2),
        out_shape=jax.ShapeDtypeStruct((b, n2, n1, width), BF16),
        grid=(b, n1 // DFT_ROWS),
        in_specs=[pl.BlockSpec((None, n2, 2, DFT_ROWS, width), lambda i, j: (i, 0, 0, j, 0)),
                  _resident(m.shape), _resident(c.shape)],
        out_specs=pl.BlockSpec((None, n2, DFT_ROWS, width), lambda i, j: (i, 0, j, 0)),
        scratch_shapes=[pltpu.VMEM((2,) + flat_shape, F32), pltpu.VMEM(flat_shape, F32)],
        compiler_params=_params(2),
        name="dft_cols",
    )(y, m, c)


def _dft_tables(s):
    n2 = DFT_N2
    n1 = s // n2
    two_pi = 2.0 * jnp.pi
    scale = (FFT_GROUP_DIM * s) ** -0.5
    i1 = jnp.arange(n1, dtype=jnp.int32)
    ang1 = ((i1[:, None] * i1[None, :]) % n1).astype(F32) * (two_pi / n1)
    angt = (i1[None, :] * jnp.arange(n2, dtype=jnp.int32)[:, None]).astype(F32) * (two_pi / s)
    wr, wi = jnp.cos(ang1)[None], -jnp.sin(ang1)[None]
    tr, ti = (jnp.cos(angt) * scale)[:, :, None], (-jnp.sin(angt) * scale)[:, :, None]
    g = jnp.concatenate([wr * tr - wi * ti, wr * ti + wi * tr], axis=1)
    idx = jnp.arange(n2, dtype=jnp.int32)
    ang2 = ((idx[:, None] * idx[None, :]) % n2).astype(F32) * (two_pi / n2)
    c2, s2 = jnp.cos(ang2), jnp.sin(ang2)
    m = jnp.concatenate([jnp.concatenate([c2, s2], axis=1), jnp.concatenate([-s2, c2], axis=1)], axis=0)
    idc = jnp.arange(FFT_GROUP_DIM, dtype=jnp.int32)
    angc = ((idc[:, None] * idc[None, :]) % FFT_GROUP_DIM).astype(F32) * (two_pi / FFT_GROUP_DIM)
    c = jnp.concatenate([jnp.cos(angc), jnp.sin(angc)], axis=0)
    return g.astype(BF16), m.astype(BF16), c.astype(BF16)


def _mixer_out_kernel(ldf_ref, ldb_ref, x_ref, g_ref, q_ref, k_ref, v_ref, gs_ref, st_ref, mask_ref, dec_ref,
                      fr_ref, wg_ref, wr_ref, wf_ref, wm_ref, o_ref, ret_ref, merged_ref, *, d, tm, rc, layer):
    _retention_tile(ldf_ref, ldb_ref, q_ref, k_ref, v_ref, gs_ref, st_ref, mask_ref, dec_ref, ret_ref,
                    tm=tm, rc=rc, layer=layer)
    x = x_ref[...]
    xg = (x * g_ref[...]).astype(BF16)
    scale = _rms_scale(x)
    for lo in range(0, d, MIX_CHUNK):
        cols = slice(lo, lo + MIX_CHUNK)
        gate_ret = jax.nn.sigmoid(_dot(xg, wg_ref[:, lo:lo + MIX_CHUNK]) * scale)
        gate_fft = jax.nn.sigmoid(_dot(xg, wg_ref[:, d + lo:d + lo + MIX_CHUNK]) * scale)
        merged = gate_ret * _dot(ret_ref[...], wr_ref[:, cols]) + gate_fft * _dot(fr_ref[...], wf_ref[:, cols])
        merged_ref[:, cols] = merged.astype(BF16)
    o_ref[...] = x + _dot(merged_ref[...], wm_ref[...])


def _mixer_out(x, g, q, k, v, gs, st, mask, dec, ld_f, ld_b, fr, w_gates, w_ret, w_fft, w_mix, layer):
    b, s, d = x.shape
    tm = TOKEN_TILE
    rc = mask.shape[-1]
    tok = lambda width: pl.BlockSpec((None, tm, width), lambda i, j: (i, j, 0))
    st_spec = pl.BlockSpec((None, None, 2 * RET_HEADS, HEAD_DIM, HEAD_DIM), lambda i, j: (i, j, 0, 0, 0))
    resident = lambda a: _layer_resident(a.shape, layer)
    return pl.pallas_call(
        functools.partial(_mixer_out_kernel, d=d, tm=tm, rc=rc, layer=layer),
        out_shape=jax.ShapeDtypeStruct((b, s, d), F32),
        grid=(b, s // tm),
        in_specs=[_smem(), _smem(), tok(d), resident(g), tok(RET_WIDTH), tok(RET_WIDTH), tok(RET_WIDTH),
                  tok(RET_WIDTH), st_spec, resident(mask), resident(dec), tok(FFT_WIDTH),
                  resident(w_gates), resident(w_ret), resident(w_fft), resident(w_mix)],
        out_specs=tok(d),
        scratch_shapes=[pltpu.VMEM((tm, RET_WIDTH), BF16), pltpu.VMEM((tm, d), BF16)],
        compiler_params=_params(2),
        name="mixer_out",
    )(ld_f, ld_b, x, g, q, k, v, gs, st, mask, dec, fr, w_gates, w_ret, w_fft, w_mix)


def _rotary_tables(s):
    half = HEAD_DIM // 2
    inv_freq = 1.0 / (ROPE_BASE ** jnp.linspace(0.0, 1.0, half, dtype=F32))
    freq = jnp.concatenate([inv_freq, inv_freq])[None, :]
    sign = jnp.concatenate([-jnp.ones((half,), F32), jnp.ones((half,), F32)])[None, :]
    starts = jnp.arange(0, s, TOKEN_TILE, dtype=F32)[:, None] * freq
    base = jnp.stack([jnp.cos(starts), jnp.sin(starts)], axis=1)[:, :, None, :]
    base = jnp.broadcast_to(base, (base.shape[0], 2, V7X_SUBLANES_F32, HEAD_DIM))
    rows = jnp.arange(TOKEN_TILE, dtype=F32)[:, None] * freq
    cos_r, sin_r = jnp.cos(rows), jnp.sin(rows)
    return base, jnp.stack([cos_r, sin_r, cos_r * sign, sin_r * sign])


def _prepare_weights(g_ffn1, w_ffn1_in, w_ffn1_out, g_mix, w_in, w_ret_out, w_fft_out, w_mix_out,
                     g_ffn2, w_ffn2_in, w_ffn2_out, g_final):
    depth, d, _ = w_in.shape
    n_proj = 4 * RET_WIDTH + FFT_WIDTH
    w_qk = w_in[:, :, :2 * RET_WIDTH].astype(BF16).reshape(depth, d, 2 * RET_HEADS, HEAD_DIM // 2, 2)
    w_qk = jnp.swapaxes(w_qk, 3, 4).reshape(depth, d, 2 * RET_WIDTH)
    w_proj = jnp.concatenate([w_qk, w_in[:, :, 2 * RET_WIDTH:n_proj].astype(BF16)], axis=2)
    gain = lambda g: g[:, None, :]
    return dict(
        g_ffn1=gain(g_ffn1), w_ffn1_in=w_ffn1_in.astype(BF16), w_ffn1_out=w_ffn1_out.astype(BF16),
        g_mix=gain(g_mix), w_proj=w_proj, w_gates=w_in[:, :, n_proj:].astype(BF16),
        w_ret_out=w_ret_out.astype(BF16), w_fft_out=w_fft_out.astype(BF16), w_mix_out=w_mix_out.astype(BF16),
        g_ffn2=gain(g_ffn2), w_ffn2_in=w_ffn2_in.astype(BF16), w_ffn2_out=w_ffn2_out.astype(BF16),
        g_final=g_final[None, None, :],
    )


def _trunk(x, w, ld_f, ld_b, mask, dec):
    b, s, d = x.shape
    assert s % TOKEN_TILE == 0 and TOKEN_TILE % RET_CHUNK == 0 and s % (DFT_N2 * DFT_ROWS) == 0
    n1 = s // DFT_N2
    depth = ld_f.shape[0]
    rot_base, rot_rows = _rotary_tables(s)
    dft_g, dft_m, dft_c = _dft_tables(s)
    x = x.reshape(b * s, d)
    for l in range(depth):
        x = _ffn(x, w["g_ffn1"], w["w_ffn1_in"], w["w_ffn1_out"], w["g_final"], l, final_norm=False)
        q, k, v, gs, f, kv = _mixer_in(x.reshape(b, s, d), w["g_mix"], w["w_proj"], rot_base, rot_rows,
                                       ld_f, ld_b, l)
        st = _ret_states(kv, ld_f, ld_b, l)
        fr = _dft_cols(_dft_rows(f.reshape(b, n1, DFT_N2, FFT_WIDTH), dft_g), dft_m, dft_c)
        x = _mixer_out(x.reshape(b, s, d), w["g_mix"], q, k, v, gs, st, mask, dec, ld_f, ld_b,
                       fr.reshape(b, s, FFT_WIDTH), w["w_gates"], w["w_ret_out"], w["w_fft_out"], w["w_mix_out"], l)
        x = _ffn(x.reshape(b * s, d), w["g_ffn2"], w["w_ffn2_in"], w["w_ffn2_out"], w["g_final"], l,
                 final_norm=(l == depth - 1))
    return x.reshape(b, s, d)


def kernel(x_prompt, x_sample, g_ffn1, w_ffn1_in, w_ffn1_out, g_mix, w_in, ret_log_decay_fwd, ret_log_decay_bwd,
           w_ret_out, w_fft_out, w_mix_out, g_ffn2, w_ffn2_in, w_ffn2_out, g_final):
    w = _prepare_weights(g_ffn1, w_ffn1_in, w_ffn1_out, g_mix, w_in, w_ret_out, w_fft_out, w_mix_out,
                         g_ffn2, w_ffn2_in, w_ffn2_out, g_final)
    mask, dec = _ret_decay(ret_log_decay_fwd, ret_log_decay_bwd, RET_CHUNK)
    return tuple(_trunk(x, w, ret_log_decay_fwd, ret_log_decay_bwd, mask, dec) for x in (x_prompt, x_sample))
```

```python
import functools

import jax
import jax.numpy as jnp
from jax import lax
from jax.experimental import pallas as pl
from jax.experimental.pallas import tpu as pltpu

F32 = jnp.float32
BF16 = jnp.bfloat16

RET_HEADS = 4
HEAD_DIM = 128
RET_WIDTH = RET_HEADS * HEAD_DIM
FFT_GROUPS = 4
FFT_GROUP_DIM = 128
FFT_WIDTH = FFT_GROUPS * FFT_GROUP_DIM
RMS_EPS = 1e-6
GN_EPS = 1e-6
ROPE_BASE = 10000.0

V7X_LANES = 128
V7X_SUBLANES_F32 = 8
V7X_SUBLANES_BF16 = 16
V7X_MXU_COLUMNS = 256
V7X_VMEM_BYTES = 64 * 1024 * 1024
VMEM_LIMIT_BYTES = V7X_VMEM_BYTES - 8 * 1024 * 1024

TOKEN_TILE = 1024
RET_CHUNK = 256
FF_CHUNK = V7X_MXU_COLUMNS
MIX_CHUNK = V7X_MXU_COLUMNS
DFT_N2 = 128
DFT_ROWS = 2 * V7X_SUBLANES_BF16


def _params(n_grid_dims):
    return pltpu.CompilerParams(
        dimension_semantics=("parallel",) * n_grid_dims,
        vmem_limit_bytes=VMEM_LIMIT_BYTES,
    )


def _resident(shape):
    return pl.BlockSpec(shape, lambda *_: (0,) * len(shape), pipeline_mode=pl.Buffered(1))


def _layer_resident(shape, layer):
    index = (layer,) + (0,) * (len(shape) - 1)
    return pl.BlockSpec((None,) + tuple(shape[1:]), lambda *_: index, pipeline_mode=pl.Buffered(1))


def _smem():
    return pl.BlockSpec(memory_space=pltpu.SMEM)


def _rms_scale(x):
    return lax.rsqrt(jnp.mean(x * x, axis=-1, keepdims=True) + RMS_EPS)


def _rmsnorm(x, g):
    return x * _rms_scale(x) * g


def _dot(a, b):
    return jnp.dot(a, b, preferred_element_type=F32)


def _ffn_kernel(x_ref, g_ref, win_ref, wout_ref, gfin_ref, o_ref, h_ref, *, d_ff, final_norm):
    x = x_ref[...]
    xg = (x * g_ref[...]).astype(BF16)
    scale = _rms_scale(x)
    for lo in range(0, d_ff, FF_CHUNK):
        a = _dot(xg, win_ref[:, lo:lo + FF_CHUNK]) * scale
        b = _dot(xg, win_ref[:, d_ff + lo:d_ff + lo + FF_CHUNK]) * scale
        h_ref[:, lo:lo + FF_CHUNK] = (jax.nn.silu(a) * b).astype(BF16)
    out = x + 0.5 * _dot(h_ref[...], wout_ref[...])
    if final_norm:
        out = _rmsnorm(out, gfin_ref[...])
    o_ref[...] = out


def _ffn(x, g, w_in, w_out, g_final, layer, *, final_norm):
    n_tok, d = x.shape
    d_ff = w_out.shape[1]
    tm = TOKEN_TILE
    assert n_tok % tm == 0 and d_ff % FF_CHUNK == 0
    tile = pl.BlockSpec((tm, d), lambda i: (i, 0))
    return pl.pallas_call(
        functools.partial(_ffn_kernel, d_ff=d_ff, final_norm=final_norm),
        out_shape=jax.ShapeDtypeStruct((n_tok, d), F32),
        grid=(n_tok // tm,),
        in_specs=[tile, _layer_resident(g.shape, layer), _layer_resident(w_in.shape, layer),
                  _layer_resident(w_out.shape, layer), _layer_resident(g_final.shape, 0)],
        out_specs=tile,
        scratch_shapes=[pltpu.VMEM((tm, d_ff), BF16)],
        compiler_params=_params(1),
        name="ffn",
    )(x, g, w_in, w_out, g_final)


def _mixer_in_kernel(ldf_ref, ldb_ref, x_ref, g_ref, w_ref, base_ref, rot_ref,
                     q_ref, k_ref, v_ref, gs_ref, f_ref, kv_ref, *, tm, layer):
    x = x_ref[...]
    xg = (x * g_ref[...]).astype(BF16)
    scale = _rms_scale(x)
    cos_b, sin_b = base_ref[0, 0:1, :], base_ref[1, 0:1, :]
    cos = cos_b * rot_ref[0] - sin_b * rot_ref[1]
    sin = sin_b * rot_ref[2] + cos_b * rot_ref[3]
    q = _dot(xg, w_ref[:, 0:RET_WIDTH]) * scale
    k = _dot(xg, w_ref[:, RET_WIDTH:2 * RET_WIDTH]) * scale
    v = (_dot(xg, w_ref[:, 2 * RET_WIDTH:3 * RET_WIDTH]) * scale).astype(BF16)
    v_ref[...] = v
    gs_ref[...] = (_dot(xg, w_ref[:, 3 * RET_WIDTH:4 * RET_WIDTH]) * scale).astype(BF16)
    f_ref[...] = (_dot(xg, w_ref[:, 4 * RET_WIDTH:4 * RET_WIDTH + FFT_WIDTH]) * scale).astype(BF16)
    row = lax.broadcasted_iota(jnp.int32, (tm, HEAD_DIM), 0).astype(F32)
    k_scale = HEAD_DIM ** -0.5
    for h in range(RET_HEADS):
        cols = slice(h * HEAD_DIM, (h + 1) * HEAD_DIM)
        qh = q[:, cols]
        q_ref[:, cols] = (qh * cos + pltpu.roll(qh, HEAD_DIM // 2, axis=1) * sin).astype(BF16)
        kh = k[:, cols]
        kr = (kh * cos + pltpu.roll(kh, HEAD_DIM // 2, axis=1) * sin) * k_scale
        k_ref[:, cols] = kr.astype(BF16)
        vh = v[:, cols]
        kf = (kr * jnp.exp(ldf_ref[layer, h] * (tm - 1.0 - row))).astype(BF16)
        kb = (kr * jnp.exp(ldb_ref[layer, h] * row)).astype(BF16)
        tn = (((0,), (0,)), ((), ()))
        kv_ref[2 * h] = lax.dot_general(kf, vh, tn, preferred_element_type=F32)
        kv_ref[2 * h + 1] = lax.dot_general(kb, vh, tn, preferred_element_type=F32)


def _mixer_in(x, g, w, rot_base, rot_rows, ld_f, ld_b, layer):
    b, s, d = x.shape
    tm = TOKEN_TILE
    nt = s // tm
    tok = lambda width: pl.BlockSpec((None, tm, width), lambda i, j: (i, j, 0))
    base = pl.BlockSpec((None,) + rot_base.shape[1:], lambda i, j: (j, 0, 0, 0))
    kv_spec = pl.BlockSpec((None, None, 2 * RET_HEADS, HEAD_DIM, HEAD_DIM), lambda i, j: (i, j, 0, 0, 0))
    act = lambda dtype: jax.ShapeDtypeStruct((b, s, RET_WIDTH), dtype)
    return pl.pallas_call(
        functools.partial(_mixer_in_kernel, tm=tm, layer=layer),
        out_shape=(act(BF16), act(BF16), act(BF16), act(BF16), act(BF16),
                   jax.ShapeDtypeStruct((b, nt, 2 * RET_HEADS, HEAD_DIM, HEAD_DIM), F32)),
        grid=(b, nt),
        in_specs=[_smem(), _smem(), tok(d), _layer_resident(g.shape, layer), _layer_resident(w.shape, layer),
                  base, _resident(rot_rows.shape)],
        out_specs=(tok(RET_WIDTH), tok(RET_WIDTH), tok(RET_WIDTH), tok(RET_WIDTH), tok(FFT_WIDTH), kv_spec),
        compiler_params=_params(2),
        name="mixer_in",
    )(ld_f, ld_b, x, g, w, rot_base, rot_rows)


def _ret_states_kernel(ldf_ref, ldb_ref, kv_ref, st_ref, *, nt, tm, layer):
    for h in range(RET_HEADS):
        a_f = jnp.exp(jnp.full((HEAD_DIM, HEAD_DIM), ldf_ref[layer, h] * tm, F32))
        a_b = jnp.exp(jnp.full((HEAD_DIM, HEAD_DIM), ldb_ref[layer, h] * tm, F32))
        state = jnp.zeros((HEAD_DIM, HEAD_DIM), F32)
        for t in range(nt):
            st_ref[t, 2 * h] = state
            state = a_f * state + kv_ref[t, 2 * h]
        state = jnp.zeros((HEAD_DIM, HEAD_DIM), F32)
        for t in reversed(range(nt)):
            st_ref[t, 2 * h + 1] = state
            state = a_b * state + kv_ref[t, 2 * h + 1]


def _ret_states(kv, ld_f, ld_b, layer):
    b, nt = kv.shape[:2]
    spec = pl.BlockSpec((None,) + kv.shape[1:], lambda i: (i, 0, 0, 0, 0))
    return pl.pallas_call(
        functools.partial(_ret_states_kernel, nt=nt, tm=TOKEN_TILE, layer=layer),
        out_shape=jax.ShapeDtypeStruct(kv.shape, F32),
        grid=(b,),
        in_specs=[_smem(), _smem(), spec],
        out_specs=spec,
        compiler_params=_params(1),
        name="ret_states",
    )(ld_f, ld_b, kv)


def _ret_decay_kernel(ldf_ref, ldb_ref, mask_ref, dec_ref, *, rc):
    layer = pl.program_id(0)
    i = lax.broadcasted_iota(jnp.int32, (rc, rc), 0)
    j = lax.broadcasted_iota(jnp.int32, (rc, rc), 1)
    dist = (i - j).astype(F32)
    row = lax.broadcasted_iota(jnp.int32, (rc, HEAD_DIM), 0).astype(F32)
    for h in range(RET_HEADS):
        ld_f = ldf_ref[layer, h]
        ld_b = ldb_ref[layer, h]
        mask_ref[h] = jnp.exp(jnp.where(i >= j, ld_f * dist, -ld_b * dist))
        dec_ref[h, 0] = jnp.exp(ld_f * (row + 1.0)).astype(BF16)
        dec_ref[h, 1] = jnp.exp(ld_b * (rc - row)).astype(BF16)
        dec_ref[h, 2] = jnp.exp(ld_f * (rc - 1.0 - row)).astype(BF16)
        dec_ref[h, 3] = jnp.exp(ld_b * row).astype(BF16)


def _ret_decay(ld_f, ld_b, rc):
    depth = ld_f.shape[0]
    return pl.pallas_call(
        functools.partial(_ret_decay_kernel, rc=rc),
        out_shape=(jax.ShapeDtypeStruct((depth, RET_HEADS, rc, rc), F32),
                   jax.ShapeDtypeStruct((depth, RET_HEADS, 4, rc, HEAD_DIM), BF16)),
        grid=(depth,),
        in_specs=[_smem(), _smem()],
        out_specs=(pl.BlockSpec((None, RET_HEADS, rc, rc), lambda l: (l, 0, 0, 0)),
                   pl.BlockSpec((None, RET_HEADS, 4, rc, HEAD_DIM), lambda l: (l, 0, 0, 0, 0))),
        compiler_params=_params(1),
        name="ret_decay",
    )(ld_f, ld_b)


def _retention_tile(ldf_ref, ldb_ref, q_ref, k_ref, v_ref, gs_ref, st_ref, mask_ref, dec_ref, o_ref,
                    *, tm, rc, layer):
    nc = tm // rc
    tn = (((0,), (0,)), ((), ()))
    nt_dims = (((1,), (1,)), ((), ()))
    for h in range(RET_HEADS):
        cols = slice(h * HEAD_DIM, (h + 1) * HEAD_DIM)
        q_dec_f, q_dec_b, k_dec_f, k_dec_b = (dec_ref[h, n] for n in range(4))
        rows = [slice(c * rc, (c + 1) * rc) for c in range(nc)]
        fwd = [st_ref[2 * h]]
        bwd = [st_ref[2 * h + 1]]
        if nc > 1:
            a_f = jnp.exp(jnp.full((HEAD_DIM, HEAD_DIM), ldf_ref[layer, h] * rc, F32))
            a_b = jnp.exp(jnp.full((HEAD_DIM, HEAD_DIM), ldb_ref[layer, h] * rc, F32))
            for c in range(nc - 1):
                kv = lax.dot_general(k_ref[rows[c], cols] * k_dec_f, v_ref[rows[c], cols], tn,
                                     preferred_element_type=F32)
                fwd.append(a_f * fwd[-1] + kv)
            for c in range(nc - 1, 0, -1):
                kv = lax.dot_general(k_ref[rows[c], cols] * k_dec_b, v_ref[rows[c], cols], tn,
                                     preferred_element_type=F32)
                bwd.append(a_b * bwd[-1] + kv)
            bwd = bwd[::-1]
        for c in range(nc):
            qc = q_ref[rows[c], cols]
            kc = k_ref[rows[c], cols]
            vc = v_ref[rows[c], cols]
            scores = lax.dot_general(qc, kc, nt_dims, preferred_element_type=F32)
            intra = _dot((scores * mask_ref[h]).astype(BF16), vc)
            q_both = jnp.concatenate([qc * q_dec_f, qc * q_dec_b], axis=1)
            s_both = jnp.concatenate([fwd[c], bwd[c]], axis=0).astype(BF16)
            y = intra + _dot(q_both, s_both)
            yc = y - jnp.mean(y, axis=-1, keepdims=True)
            yn = yc * lax.rsqrt(jnp.mean(yc * yc, axis=-1, keepdims=True) + GN_EPS)
            o_ref[rows[c], cols] = (jax.nn.silu(gs_ref[rows[c], cols].astype(F32)) * yn).astype(BF16)


def _dft_rows_kernel(*refs, n1, t_rows):
    f_refs, g_ref, y_ref, flat_ref = refs[:FFT_GROUPS], refs[FFT_GROUPS], refs[-2], refs[-1]
    sub = V7X_SUBLANES_F32
    for grp, f_ref in enumerate(f_refs):
        f = f_ref[...].astype(F32)
        for part in range(t_rows // sub):
            flat_ref[grp, part] = f[:, part * sub:(part + 1) * sub, :].reshape(n1 * sub, FFT_GROUP_DIM)
    for t in range(t_rows):
        pick = pl.ds(t % sub, n1, stride=sub)
        x = jnp.concatenate([flat_ref[grp, t // sub, pick, :] for grp in range(FFT_GROUPS)], axis=1).astype(BF16)
        y_ref[t] = _dot(g_ref[t], x).astype(BF16)


def _dft_rows(f, g):
    b, n1, n2, width = f.shape
    t_rows = DFT_ROWS
    grp = [pl.BlockSpec((None, n1, t_rows, FFT_GROUP_DIM), functools.partial(lambda i, j, c: (i, 0, j, c), c=c))
           for c in range(FFT_GROUPS)]
    return pl.pallas_call(
        functools.partial(_dft_rows_kernel, n1=n1, t_rows=t_rows),
        out_shape=jax.ShapeDtypeStruct((b, n2, 2 * n1, width), BF16),
        grid=(b, n2 // t_rows),
        in_specs=grp + [pl.BlockSpec((t_rows, 2 * n1, n1), lambda i, j: (j, 0, 0))],
        out_specs=pl.BlockSpec((None, t_rows, 2 * n1, width), lambda i, j: (i, j, 0, 0)),
        scratch_shapes=[pltpu.VMEM((FFT_GROUPS, t_rows // V7X_SUBLANES_F32, n1 * V7X_SUBLANES_F32, FFT_GROUP_DIM),
                                   F32)],
        compiler_params=_params(2),
        name="dft_rows",
    )(*([f] * FFT_GROUPS), g)


def _dft_cols_kernel(y_ref, m_ref, c_ref, o_ref, yflat_ref, oflat_ref, *, n2):
    sub = V7X_SUBLANES_F32
    halves = DFT_ROWS // sub
    group_cols = [slice(grp * FFT_GROUP_DIM, (grp + 1) * FFT_GROUP_DIM) for grp in range(FFT_GROUPS)]
    for part in range(2):
        for grp in range(FFT_GROUPS):
            y = y_ref[:, part, :, group_cols[grp]].astype(F32)
            for half in range(halves):
                yflat_ref[part, grp, half] = y[:, half * sub:(half + 1) * sub, :].reshape(n2 * sub, FFT_GROUP_DIM)
    for t in range(DFT_ROWS):
        half, pick = t // sub, pl.ds(t % sub, n2, stride=sub)
        a = jnp.concatenate(
            [jnp.concatenate([yflat_ref[part, grp, half, pick, :] for grp in range(FFT_GROUPS)], axis=1)
             for part in range(2)], axis=0).astype(BF16)
        y = _dot(m_ref[...], a)
        y_both = jnp.concatenate([jnp.concatenate([y[:n2, cols], y[n2:, cols]], axis=1) for cols in group_cols],
                                 axis=0).astype(BF16)
        fr = _dot(y_both, c_ref[...])
        for grp in range(FFT_GROUPS):
            oflat_ref[grp, half, pick, :] = fr[grp * n2:(grp + 1) * n2]
    o_ref[...] = jnp.concatenate(
        [jnp.concatenate([oflat_ref[grp, half].reshape(n2, sub, FFT_GROUP_DIM) for half in range(halves)], axis=1)
         for grp in range(FFT_GROUPS)], axis=2).astype(BF16)


def _dft_cols(y, m, c):
    b, n2, two_n1, width = y.shape
    n1 = two_n1 // 2
    y = y.reshape(b, n2, 2, n1, width)
    flat_shape = (FFT_GROUPS, DFT_ROWS // V7X_SUBLANES_F32, n2 * V7X_SUBLANES_F32, FFT_GROUP_DIM)
    return pl.pallas_call(
        functools.partial(_dft_cols_kernel, n2=n2),
        out_shape=jax.ShapeDtypeStruct((b, n2, n1, width), BF16),
        grid=(b, n1 // DFT_ROWS),
        in_specs=[pl.BlockSpec((None, n2, 2, DFT_ROWS, width), lambda i, j: (i, 0, 0, j, 0)),
                  _resident(m.shape), _resident(c.shape)],
        out_specs=pl.BlockSpec((None, n2, DFT_ROWS, width), lambda i, j: (i, 0, j, 0)),
        scratch_shapes=[pltpu.VMEM((2,) + flat_shape, F32), pltpu.VMEM(flat_shape, F32)],
        compiler_params=_params(2),
        name="dft_cols",
    )(y, m, c)


def _dft_tables(s):
    n2 = DFT_N2
    n1 = s // n2
    two_pi = 2.0 * jnp.pi
    scale = (FFT_GROUP_DIM * s) ** -0.5
    i1 = jnp.arange(n1, dtype=jnp.int32)
    ang1 = ((i1[:, None] * i1[None, :]) % n1).astype(F32) * (two_pi / n1)
    angt = (i1[None, :] * jnp.arange(n2, dtype=jnp.int32)[:, None]).astype(F32) * (two_pi / s)
    wr, wi = jnp.cos(ang1)[None], -jnp.sin(ang1)[None]
    tr, ti = (jnp.cos(angt) * scale)[:, :, None], (-jnp.sin(angt) * scale)[:, :, None]
    g = jnp.concatenate([wr * tr - wi * ti, wr * ti + wi * tr], axis=1)
    idx = jnp.arange(n2, dtype=jnp.int32)
    ang2 = ((idx[:, None] * idx[None, :]) % n2).astype(F32) * (two_pi / n2)
    c2, s2 = jnp.cos(ang2), jnp.sin(ang2)
    m = jnp.concatenate([jnp.concatenate([c2, s2], axis=1), jnp.concatenate([-s2, c2], axis=1)], axis=0)
    idc = jnp.arange(FFT_GROUP_DIM, dtype=jnp.int32)
    angc = ((idc[:, None] * idc[None, :]) % FFT_GROUP_DIM).astype(F32) * (two_pi / FFT_GROUP_DIM)
    c = jnp.concatenate([jnp.cos(angc), jnp.sin(angc)], axis=0)
    return g.astype(BF16), m.astype(BF16), c.astype(BF16)


def _mixer_out_kernel(ldf_ref, ldb_ref, x_ref, g_ref, q_ref, k_ref, v_ref, gs_ref, st_ref, mask_ref, dec_ref,
                      fr_ref, wg_ref, wr_ref, wf_ref, wm_ref, o_ref, ret_ref, merged_ref, *, d, tm, rc, layer):
    _retention_tile(ldf_ref, ldb_ref, q_ref, k_ref, v_ref, gs_ref, st_ref, mask_ref, dec_ref, ret_ref,
                    tm=tm, rc=rc, layer=layer)
    x = x_ref[...]
    xg = (x * g_ref[...]).astype(BF16)
    scale = _rms_scale(x)
    for lo in range(0, d, MIX_CHUNK):
        cols = slice(lo, lo + MIX_CHUNK)
        gate_ret = jax.nn.sigmoid(_dot(xg, wg_ref[:, lo:lo + MIX_CHUNK]) * scale)
        gate_fft = jax.nn.sigmoid(_dot(xg, wg_ref[:, d + lo:d + lo + MIX_CHUNK]) * scale)
        merged = gate_ret * _dot(ret_ref[...], wr_ref[:, cols]) + gate_fft * _dot(fr_ref[...], wf_ref[:, cols])
        merged_ref[:, cols] = merged.astype(BF16)
    o_ref[...] = x + _dot(merged_ref[...], wm_ref[...])


def _mixer_out(x, g, q, k, v, gs, st, mask, dec, ld_f, ld_b, fr, w_gates, w_ret, w_fft, w_mix, layer):
    b, s, d = x.shape
    tm = TOKEN_TILE
    rc = mask.shape[-1]
    tok = lambda width: pl.BlockSpec((None, tm, width), lambda i, j: (i, j, 0))
    st_spec = pl.BlockSpec((None, None, 2 * RET_HEADS, HEAD_DIM, HEAD_DIM), lambda i, j: (i, j, 0, 0, 0))
    resident = lambda a: _layer_resident(a.shape, layer)
    return pl.pallas_call(
        functools.partial(_mixer_out_kernel, d=d, tm=tm, rc=rc, layer=layer),
        out_shape=jax.ShapeDtypeStruct((b, s, d), F32),
        grid=(b, s // tm),
        in_specs=[_smem(), _smem(), tok(d), resident(g), tok(RET_WIDTH), tok(RET_WIDTH), tok(RET_WIDTH),
                  tok(RET_WIDTH), st_spec, resident(mask), resident(dec), tok(FFT_WIDTH),
                  resident(w_gates), resident(w_ret), resident(w_fft), resident(w_mix)],
        out_specs=tok(d),
        scratch_shapes=[pltpu.VMEM((tm, RET_WIDTH), BF16), pltpu.VMEM((tm, d), BF16)],
        compiler_params=_params(2),
        name="mixer_out",
    )(ld_f, ld_b, x, g, q, k, v, gs, st, mask, dec, fr, w_gates, w_ret, w_fft, w_mix)


def _rotary_tables(s):
    half = HEAD_DIM // 2
    inv_freq = 1.0 / (ROPE_BASE ** jnp.linspace(0.0, 1.0, half, dtype=F32))
    freq = jnp.concatenate([inv_freq, inv_freq])[None, :]
    sign = jnp.concatenate([-jnp.ones((half,), F32), jnp.ones((half,), F32)])[None, :]
    starts = jnp.arange(0, s, TOKEN_TILE, dtype=F32)[:, None] * freq
    base = jnp.stack([jnp.cos(starts), jnp.sin(starts)], axis=1)[:, :, None, :]
    base = jnp.broadcast_to(base, (base.shape[0], 2, V7X_SUBLANES_F32, HEAD_DIM))
    rows = jnp.arange(TOKEN_TILE, dtype=F32)[:, None] * freq
    cos_r, sin_r = jnp.cos(rows), jnp.sin(rows)
    return base, jnp.stack([cos_r, sin_r, cos_r * sign, sin_r * sign])


def _prepare_weights(g_ffn1, w_ffn1_in, w_ffn1_out, g_mix, w_in, w_ret_out, w_fft_out, w_mix_out,
                     g_ffn2, w_ffn2_in, w_ffn2_out, g_final):
    depth, d, _ = w_in.shape
    n_proj = 4 * RET_WIDTH + FFT_WIDTH
    w_qk = w_in[:, :, :2 * RET_WIDTH].astype(BF16).reshape(depth, d, 2 * RET_HEADS, HEAD_DIM // 2, 2)
    w_qk = jnp.swapaxes(w_qk, 3, 4).reshape(depth, d, 2 * RET_WIDTH)
    w_proj = jnp.concatenate([w_qk, w_in[:, :, 2 * RET_WIDTH:n_proj].astype(BF16)], axis=2)
    gain = lambda g: g[:, None, :]
    return dict(
        g_ffn1=gain(g_ffn1), w_ffn1_in=w_ffn1_in.astype(BF16), w_ffn1_out=w_ffn1_out.astype(BF16),
        g_mix=gain(g_mix), w_proj=w_proj, w_gates=w_in[:, :, n_proj:].astype(BF16),
        w_ret_out=w_ret_out.astype(BF16), w_fft_out=w_fft_out.astype(BF16), w_mix_out=w_mix_out.astype(BF16),
        g_ffn2=gain(g_ffn2), w_ffn2_in=w_ffn2_in.astype(BF16), w_ffn2_out=w_ffn2_out.astype(BF16),
        g_final=g_final[None, None, :],
    )


def _trunk(x, w, ld_f, ld_b, mask, dec):
    b, s, d = x.shape
    assert s % TOKEN_TILE == 0 and TOKEN_TILE % RET_CHUNK == 0 and s % (DFT_N2 * DFT_ROWS) == 0
    n1 = s // DFT_N2
    depth = ld_f.shape[0]
    rot_base, rot_rows = _rotary_tables(s)
    dft_g, dft_m, dft_c = _dft_tables(s)
    x = x.reshape(b * s, d)
    for l in range(depth):
        x = _ffn(x, w["g_ffn1"], w["w_ffn1_in"], w["w_ffn1_out"], w["g_final"], l, final_norm=False)
        q, k, v, gs, f, kv = _mixer_in(x.reshape(b, s, d), w["g_mix"], w["w_proj"], rot_base, rot_rows,
                                       ld_f, ld_b, l)
        st = _ret_states(kv, ld_f, ld_b, l)
        fr = _dft_cols(_dft_rows(f.reshape(b, n1, DFT_N2, FFT_WIDTH), dft_g), dft_m, dft_c)
        x = _mixer_out(x.reshape(b, s, d), w["g_mix"], q, k, v, gs, st, mask, dec, ld_f, ld_b,
                       fr.reshape(b, s, FFT_WIDTH), w["w_gates"], w["w_ret_out"], w["w_fft_out"], w["w_mix_out"], l)
        x = _ffn(x.reshape(b * s, d), w["g_ffn2"], w["w_ffn2_in"], w["w_ffn2_out"], w["g_final"], l,
                 final_norm=(l == depth - 1))
    return x.reshape(b, s, d)


def kernel(x_prompt, x_sample, g_ffn1, w_ffn1_in, w_ffn1_out, g_mix, w_in, ret_log_decay_fwd, ret_log_decay_bwd,
           w_ret_out, w_fft_out, w_mix_out, g_ffn2, w_ffn2_in, w_ffn2_out, g_final):
    w = _prepare_weights(g_ffn1, w_ffn1_in, w_ffn1_out, g_mix, w_in, w_ret_out, w_fft_out, w_mix_out,
                         g_ffn2, w_ffn2_in, w_ffn2_out, g_final)
    mask, dec = _ret_decay(ret_log_decay_fwd, ret_log_decay_bwd, RET_CHUNK)
    return tuple(_trunk(x, w, ret_log_decay_fwd, ret_log_decay_bwd, mask, dec) for x in (x_prompt, x_sample))
```

```python
import functools

import jax
import jax.numpy as jnp
from jax import lax
from jax.experimental import pallas as pl
from jax.experimental.pallas import tpu as pltpu

F32 = jnp.float32
BF16 = jnp.bfloat16

RET_HEADS = 4
HEAD_DIM = 128
RET_WIDTH = RET_HEADS * HEAD_DIM
FFT_GROUPS = 4
FFT_GROUP_DIM = 128
FFT_WIDTH = FFT_GROUPS * FFT_GROUP_DIM
RMS_EPS = 1e-6
GN_EPS = 1e-6
ROPE_BASE = 10000.0

V7X_SUBLANES_F32 = 8
V7X_SUBLANES_BF16 = 16
V7X_MXU_COLUMNS = 256
V7X_VMEM_BYTES = 64 * 1024 * 1024
VMEM_LIMIT_BYTES = V7X_VMEM_BYTES - 8 * 1024 * 1024

TOKEN_TILE = 1024
RET_CHUNK = 256
FF_CHUNK = V7X_MXU_COLUMNS
MIX_CHUNK = V7X_MXU_COLUMNS
DFT_N2 = 128
DFT_ROWS = 2 * V7X_SUBLANES_BF16


def _params(n_grid_dims):
    return pltpu.CompilerParams(
        dimension_semantics=("parallel",) * n_grid_dims,
        vmem_limit_bytes=VMEM_LIMIT_BYTES,
    )


def _resident(shape):
    return pl.BlockSpec(shape, lambda *_: (0,) * len(shape), pipeline_mode=pl.Buffered(1))


def _layer_resident(shape, layer):
    index = (layer,) + (0,) * (len(shape) - 1)
    return pl.BlockSpec((None,) + tuple(shape[1:]), lambda *_: index, pipeline_mode=pl.Buffered(1))


def _smem():
    return pl.BlockSpec(memory_space=pltpu.SMEM)


def _rms_scale(x):
    return lax.rsqrt(jnp.mean(x * x, axis=-1, keepdims=True) + RMS_EPS)


def _rmsnorm(x, g):
    return x * _rms_scale(x) * g


def _dot(a, b):
    return jnp.dot(a, b, preferred_element_type=F32)


def _ffn_kernel(x_ref, g_ref, win_ref, wout_ref, gfin_ref, o_ref, h_ref, *, d_ff, final_norm):
    x = x_ref[...]
    xg = (x * g_ref[...]).astype(BF16)
    scale = _rms_scale(x)
    for lo in range(0, d_ff, FF_CHUNK):
        a = _dot(xg, win_ref[:, lo:lo + FF_CHUNK]) * scale
        b = _dot(xg, win_ref[:, d_ff + lo:d_ff + lo + FF_CHUNK]) * scale
        h_ref[:, lo:lo + FF_CHUNK] = (jax.nn.silu(a) * b).astype(BF16)
    out = x + 0.5 * _dot(h_ref[...], wout_ref[...])
    if final_norm:
        out = _rmsnorm(out, gfin_ref[...])
    o_ref[...] = out


def _ffn(x, g, w_in, w_out, g_final, layer, *, final_norm):
    n_tok, d = x.shape
    d_ff = w_out.shape[1]
    tm = TOKEN_TILE
    assert n_tok % tm == 0 and d_ff % FF_CHUNK == 0
    tile = pl.BlockSpec((tm, d), lambda i: (i, 0))
    return pl.pallas_call(
        functools.partial(_ffn_kernel, d_ff=d_ff, final_norm=final_norm),
        out_shape=jax.ShapeDtypeStruct((n_tok, d), F32),
        grid=(n_tok // tm,),
        in_specs=[tile, _layer_resident(g.shape, layer), _layer_resident(w_in.shape, layer),
                  _layer_resident(w_out.shape, layer), _layer_resident(g_final.shape, 0)],
        out_specs=tile,
        scratch_shapes=[pltpu.VMEM((tm, d_ff), BF16)],
        compiler_params=_params(1),
        name="ffn",
    )(x, g, w_in, w_out, g_final)


def _mixer_in_kernel(ldf_ref, ldb_ref, x_ref, g_ref, w_ref, base_ref, rot_ref,
                     q_ref, k_ref, v_ref, gs_ref, f_ref, kv_ref, *, tm, layer):
    x = x_ref[...]
    xg = (x * g_ref[...]).astype(BF16)
    scale = _rms_scale(x)
    cos_b, sin_b = base_ref[0, 0:1, :], base_ref[1, 0:1, :]
    cos = cos_b * rot_ref[0] - sin_b * rot_ref[1]
    sin = sin_b * rot_ref[2] + cos_b * rot_ref[3]
    q = _dot(xg, w_ref[:, 0:RET_WIDTH]) * scale
    k = _dot(xg, w_ref[:, RET_WIDTH:2 * RET_WIDTH]) * scale
    v = (_dot(xg, w_ref[:, 2 * RET_WIDTH:3 * RET_WIDTH]) * scale).astype(BF16)
    v_ref[...] = v
    gs_ref[...] = (_dot(xg, w_ref[:, 3 * RET_WIDTH:4 * RET_WIDTH]) * scale).astype(BF16)
    f_ref[...] = (_dot(xg, w_ref[:, 4 * RET_WIDTH:4 * RET_WIDTH + FFT_WIDTH]) * scale).astype(BF16)
    row = lax.broadcasted_iota(jnp.int32, (tm, HEAD_DIM), 0).astype(F32)
    k_scale = HEAD_DIM ** -0.5
    for h in range(RET_HEADS):
        cols = slice(h * HEAD_DIM, (h + 1) * HEAD_DIM)
        qh = q[:, cols]
        q_ref[:, cols] = (qh * cos + pltpu.roll(qh, HEAD_DIM // 2, axis=1) * sin).astype(BF16)
        kh = k[:, cols]
        kr = (kh * cos + pltpu.roll(kh, HEAD_DIM // 2, axis=1) * sin) * k_scale
        k_ref[:, cols] = kr.astype(BF16)
        vh = v[:, cols]
        kf = (kr * jnp.exp(ldf_ref[layer, h] * (tm - 1.0 - row))).astype(BF16)
        kb = (kr * jnp.exp(ldb_ref[layer, h] * row)).astype(BF16)
        tn = (((0,), (0,)), ((), ()))
        kv_ref[2 * h] = lax.dot_general(kf, vh, tn, preferred_element_type=F32)
        kv_ref[2 * h + 1] = lax.dot_general(kb, vh, tn, preferred_element_type=F32)


def _mixer_in(x, g, w, rot_base, rot_rows, ld_f, ld_b, layer):
    b, s, d = x.shape
    tm = TOKEN_TILE
    nt = s // tm
    tok = lambda width: pl.BlockSpec((None, tm, width), lambda i, j: (i, j, 0))
    base = pl.BlockSpec((None,) + rot_base.shape[1:], lambda i, j: (j, 0, 0, 0))
    kv_spec = pl.BlockSpec((None, None, 2 * RET_HEADS, HEAD_DIM, HEAD_DIM), lambda i, j: (i, j, 0, 0, 0))
    act = lambda dtype: jax.ShapeDtypeStruct((b, s, RET_WIDTH), dtype)
    return pl.pallas_call(
        functools.partial(_mixer_in_kernel, tm=tm, layer=layer),
        out_shape=(act(BF16), act(BF16), act(BF16), act(BF16), act(BF16),
                   jax.ShapeDtypeStruct((b, nt, 2 * RET_HEADS, HEAD_DIM, HEAD_DIM), F32)),
        grid=(b, nt),
        in_specs=[_smem(), _smem(), tok(d), _layer_resident(g.shape, layer), _layer_resident(w.shape, layer),
                  base, _resident(rot_rows.shape)],
        out_specs=(tok(RET_WIDTH), tok(RET_WIDTH), tok(RET_WIDTH), tok(RET_WIDTH), tok(FFT_WIDTH), kv_spec),
        compiler_params=_params(2),
        name="mixer_in",
    )(ld_f, ld_b, x, g, w, rot_base, rot_rows)


def _ret_states_kernel(ldf_ref, ldb_ref, kv_ref, st_ref, *, nt, tm, layer):
    for h in range(RET_HEADS):
        a_f = jnp.exp(jnp.full((HEAD_DIM, HEAD_DIM), ldf_ref[layer, h] * tm, F32))
        a_b = jnp.exp(jnp.full((HEAD_DIM, HEAD_DIM), ldb_ref[layer, h] * tm, F32))
        state = jnp.zeros((HEAD_DIM, HEAD_DIM), F32)
        for t in range(nt):
            st_ref[t, 2 * h] = state
            state = a_f * state + kv_ref[t, 2 * h]
        state = jnp.zeros((HEAD_DIM, HEAD_DIM), F32)
        for t in reversed(range(nt)):
            st_ref[t, 2 * h + 1] = state
            state = a_b * state + kv_ref[t, 2 * h + 1]


def _ret_states(kv, ld_f, ld_b, layer):
    b, nt = kv.shape[:2]
    spec = pl.BlockSpec((None,) + kv.shape[1:], lambda i: (i, 0, 0, 0, 0))
    return pl.pallas_call(
        functools.partial(_ret_states_kernel, nt=nt, tm=TOKEN_TILE, layer=layer),
        out_shape=jax.ShapeDtypeStruct(kv.shape, F32),
        grid=(b,),
        in_specs=[_smem(), _smem(), spec],
        out_specs=spec,
        compiler_params=_params(1),
        name="ret_states",
    )(ld_f, ld_b, kv)


def _ret_decay_kernel(ldf_ref, ldb_ref, mask_ref, dec_ref, *, rc):
    layer = pl.program_id(0)
    i = lax.broadcasted_iota(jnp.int32, (rc, rc), 0)
    j = lax.broadcasted_iota(jnp.int32, (rc, rc), 1)
    dist = (i - j).astype(F32)
    row = lax.broadcasted_iota(jnp.int32, (rc, HEAD_DIM), 0).astype(F32)
    for h in range(RET_HEADS):
        ld_f = ldf_ref[layer, h]
        ld_b = ldb_ref[layer, h]
        mask_ref[h] = jnp.exp(jnp.where(i >= j, ld_f * dist, -ld_b * dist))
        dec_ref[h, 0] = jnp.exp(ld_f * (row + 1.0)).astype(BF16)
        dec_ref[h, 1] = jnp.exp(ld_b * (rc - row)).astype(BF16)
        dec_ref[h, 2] = jnp.exp(ld_f * (rc - 1.0 - row)).astype(BF16)
        dec_ref[h, 3] = jnp.exp(ld_b * row).astype(BF16)


def _ret_decay(ld_f, ld_b, rc):
    depth = ld_f.shape[0]
    return pl.pallas_call(
        functools.partial(_ret_decay_kernel, rc=rc),
        out_shape=(jax.ShapeDtypeStruct((depth, RET_HEADS, rc, rc), F32),
                   jax.ShapeDtypeStruct((depth, RET_HEADS, 4, rc, HEAD_DIM), BF16)),
        grid=(depth,),
        in_specs=[_smem(), _smem()],
        out_specs=(pl.BlockSpec((None, RET_HEADS, rc, rc), lambda l: (l, 0, 0, 0)),
                   pl.BlockSpec((None, RET_HEADS, 4, rc, HEAD_DIM), lambda l: (l, 0, 0, 0, 0))),
        compiler_params=_params(1),
        name="ret_decay",
    )(ld_f, ld_b)


def _retention_tile(ldf_ref, ldb_ref, q_ref, k_ref, v_ref, gs_ref, st_ref, mask_ref, dec_ref, o_ref,
                    *, tm, rc, layer):
    nc = tm // rc
    tn = (((0,), (0,)), ((), ()))
    nt_dims = (((1,), (1,)), ((), ()))
    for h in range(RET_HEADS):
        cols = slice(h * HEAD_DIM, (h + 1) * HEAD_DIM)
        q_dec_f, q_dec_b, k_dec_f, k_dec_b = (dec_ref[h, n] for n in range(4))
        rows = [slice(c * rc, (c + 1) * rc) for c in range(nc)]
        fwd = [st_ref[2 * h]]
        bwd = [st_ref[2 * h + 1]]
        if nc > 1:
            a_f = jnp.exp(jnp.full((HEAD_DIM, HEAD_DIM), ldf_ref[layer, h] * rc, F32))
            a_b = jnp.exp(jnp.full((HEAD_DIM, HEAD_DIM), ldb_ref[layer, h] * rc, F32))
            for c in range(nc - 1):
                kv = lax.dot_general(k_ref[rows[c], cols] * k_dec_f, v_ref[rows[c], cols], tn,
                                     preferred_element_type=F32)
                fwd.append(a_f * fwd[-1] + kv)
            for c in range(nc - 1, 0, -1):
                kv = lax.dot_general(k_ref[rows[c], cols] * k_dec_b, v_ref[rows[c], cols], tn,
                                     preferred_element_type=F32)
                bwd.append(a_b * bwd[-1] + kv)
            bwd = bwd[::-1]
        for c in range(nc):
            qc = q_ref[rows[c], cols]
            kc = k_ref[rows[c], cols]
            vc = v_ref[rows[c], cols]
            scores = lax.dot_general(qc, kc, nt_dims, preferred_element_type=F32)
            intra = _dot((scores * mask_ref[h]).astype(BF16), vc)
            q_both = jnp.concatenate([qc * q_dec_f, qc * q_dec_b], axis=1)
            s_both = jnp.concatenate([fwd[c], bwd[c]], axis=0).astype(BF16)
            y = intra + _dot(q_both, s_both)
            yc = y - jnp.mean(y, axis=-1, keepdims=True)
            yn = yc * lax.rsqrt(jnp.mean(yc * yc, axis=-1, keepdims=True) + GN_EPS)
            o_ref[rows[c], cols] = (jax.nn.silu(gs_ref[rows[c], cols].astype(F32)) * yn).astype(BF16)


def _dft_rows_kernel(*refs, n1, t_rows):
    f_refs, g_ref, y_ref, flat_ref = refs[:FFT_GROUPS], refs[FFT_GROUPS], refs[-2], refs[-1]
    sub = V7X_SUBLANES_F32
    for grp, f_ref in enumerate(f_refs):
        f = f_ref[...].astype(F32)
        for part in range(t_rows // sub):
            flat_ref[grp, part] = f[:, part * sub:(part + 1) * sub, :].reshape(n1 * sub, FFT_GROUP_DIM)
    for t in range(t_rows):
        pick = pl.ds(t % sub, n1, stride=sub)
        x = jnp.concatenate([flat_ref[grp, t // sub, pick, :] for grp in range(FFT_GROUPS)], axis=1).astype(BF16)
        y_ref[t] = _dot(g_ref[t], x).astype(BF16)


def _dft_rows(f, g):
    b, n1, n2, width = f.shape
    t_rows = DFT_ROWS
    grp = [pl.BlockSpec((None, n1, t_rows, FFT_GROUP_DIM), functools.partial(lambda i, j, c: (i, 0, j, c), c=c))
           for c in range(FFT_GROUPS)]
    return pl.pallas_call(
        functools.partial(_dft_rows_kernel, n1=n1, t_rows=t_rows),
        out_shape=jax.ShapeDtypeStruct((b, n2, 2 * n1, width), BF16),
        grid=(b, n2 // t_rows),
        in_specs=grp + [pl.BlockSpec((t_rows, 2 * n1, n1), lambda i, j: (j, 0, 0))],
        out_specs=pl.BlockSpec((None, t_rows, 2 * n1, width), lambda i, j: (i, j, 0, 0)),
        scratch_shapes=[pltpu.VMEM((FFT_GROUPS, t_rows // V7X_SUBLANES_F32, n1 * V7X_SUBLANES_F32, FFT_GROUP_DIM),
                                   F32)],
        compiler_params=_params(2),
        name="dft_rows",
    )(*([f] * FFT_GROUPS), g)


def _dft_cols_kernel(y_ref, m_ref, c_ref, o_ref, yflat_ref, oflat_ref, *, n2):
    sub = V7X_SUBLANES_F32
    halves = DFT_ROWS // sub
    group_cols = [slice(grp * FFT_GROUP_DIM, (grp + 1) * FFT_GROUP_DIM) for grp in range(FFT_GROUPS)]
    for part in range(2):
        for grp in range(FFT_GROUPS):
            y = y_ref[:, part, :, group_cols[grp]].astype(F32)
            for half in range(halves):
                yflat_ref[part, grp, half] = y[:, half * sub:(half + 1) * sub, :].reshape(n2 * sub, FFT_GROUP_DIM)
    for t in range(DFT_ROWS):
        half, pick = t // sub, pl.ds(t % sub, n2, stride=sub)
        a = jnp.concatenate(
            [jnp.concatenate([yflat_ref[part, grp, half, pick, :] for grp in range(FFT_GROUPS)], axis=1)
             for part in range(2)], axis=0).astype(BF16)
        y = _dot(m_ref[...], a)
        y_both = jnp.concatenate([jnp.concatenate([y[:n2, cols], y[n2:, cols]], axis=1) for cols in group_cols],
                                 axis=0).astype(BF16)
        fr = _dot(y_both, c_ref[...])
        for grp in range(FFT_GROUPS):
            oflat_ref[grp, half, pick, :] = fr[grp * n2:(grp + 1) * n2]
    o_ref[...] = jnp.concatenate(
        [jnp.concatenate([oflat_ref[grp, half].reshape(n2, sub, FFT_GROUP_DIM) for half in range(halves)], axis=1)
         for grp in range(FFT_GROUPS)], axis=2).astype(BF16)


def _dft_cols(y, m, c):
    b, n2, two_n1, width = y.shape
    n1 = two_n1 // 2
    y = y.reshape(b, n2, 2, n1, width)
    flat_shape = (FFT_GROUPS, DFT_ROWS // V7X_SUBLANES_F32, n2 * V7X_SUBLANES_F32, FFT_GROUP_DIM)
    return pl.pallas_call(
        functools.partial(_dft_cols_kernel, n2=n2),
        out_shape=jax.ShapeDtypeStruct((b, n2, n1, width), BF16),
        grid=(b, n1 // DFT_ROWS),
        in_specs=[pl.BlockSpec((None, n2, 2, DFT_ROWS, width), lambda i, j: (i, 0, 0, j, 0)),
                  _resident(m.shape), _resident(c.shape)],
        out_specs=pl.BlockSpec((None, n2, DFT_ROWS, width), lambda i, j: (i, 0, j, 0)),
        scratch_shapes=[pltpu.VMEM((2,) + flat_shape, F32), pltpu.VMEM(flat_shape, F32)],
        compiler_params=_params(2),
        name="dft_cols",
    )(y, m, c)


def _dft_tables(s):
    n2 = DFT_N2
    n1 = s // n2
    two_pi = 2.0 * jnp.pi
    scale = (FFT_GROUP_DIM * s) ** -0.5
    i1 = jnp.arange(n1, dtype=jnp.int32)
    ang1 = ((i1[:, None] * i1[None, :]) % n1).astype(F32) * (two_pi / n1)
    angt = (i1[None, :] * jnp.arange(n2, dtype=jnp.int32)[:, None]).astype(F32) * (two_pi / s)
    wr, wi = jnp.cos(ang1)[None], -jnp.sin(ang1)[None]
    tr, ti = (jnp.cos(angt) * scale)[:, :, None], (-jnp.sin(angt) * scale)[:, :, None]
    g = jnp.concatenate([wr * tr - wi * ti, wr * ti + wi * tr], axis=1)
    idx = jnp.arange(n2, dtype=jnp.int32)
    ang2 = ((idx[:, None] * idx[None, :]) % n2).astype(F32) * (two_pi / n2)
    c2, s2 = jnp.cos(ang2), jnp.sin(ang2)
    m = jnp.concatenate([jnp.concatenate([c2, s2], axis=1), jnp.concatenate([-s2, c2], axis=1)], axis=0)
    idc = jnp.arange(FFT_GROUP_DIM, dtype=jnp.int32)
    angc = ((idc[:, None] * idc[None, :]) % FFT_GROUP_DIM).astype(F32) * (two_pi / FFT_GROUP_DIM)
    c = jnp.concatenate([jnp.cos(angc), jnp.sin(angc)], axis=0)
    return g.astype(BF16), m.astype(BF16), c.astype(BF16)


def _mixer_out_kernel(ldf_ref, ldb_ref, x_ref, g_ref, q_ref, k_ref, v_ref, gs_ref, st_ref, mask_ref, dec_ref,
                      fr_ref, wg_ref, wr_ref, wf_ref, wm_ref, o_ref, ret_ref, merged_ref, *, d, tm, rc, layer):
    _retention_tile(ldf_ref, ldb_ref, q_ref, k_ref, v_ref, gs_ref, st_ref, mask_ref, dec_ref, ret_ref,
                    tm=tm, rc=rc, layer=layer)
    x = x_ref[...]
    xg = (x * g_ref[...]).astype(BF16)
    scale = _rms_scale(x)
    for lo in range(0, d, MIX_CHUNK):
        cols = slice(lo, lo + MIX_CHUNK)
        gate_ret = jax.nn.sigmoid(_dot(xg, wg_ref[:, lo:lo + MIX_CHUNK]) * scale)
        gate_fft = jax.nn.sigmoid(_dot(xg, wg_ref[:, d + lo:d + lo + MIX_CHUNK]) * scale)
        merged = gate_ret * _dot(ret_ref[...], wr_ref[:, cols]) + gate_fft * _dot(fr_ref[...], wf_ref[:, cols])
        merged_ref[:, cols] = merged.astype(BF16)
    o_ref[...] = x + _dot(merged_ref[...], wm_ref[...])


def _mixer_out(x, g, q, k, v, gs, st, mask, dec, ld_f, ld_b, fr, w_gates, w_ret, w_fft, w_mix, layer):
    b, s, d = x.shape
    tm = TOKEN_TILE
    rc = mask.shape[-1]
    tok = lambda width: pl.BlockSpec((None, tm, width), lambda i, j: (i, j, 0))
    st_spec = pl.BlockSpec((None, None, 2 * RET_HEADS, HEAD_DIM, HEAD_DIM), lambda i, j: (i, j, 0, 0, 0))
    resident = lambda a: _layer_resident(a.shape, layer)
    return pl.pallas_call(
        functools.partial(_mixer_out_kernel, d=d, tm=tm, rc=rc, layer=layer),
        out_shape=jax.ShapeDtypeStruct((b, s, d), F32),
        grid=(b, s // tm),
        in_specs=[_smem(), _smem(), tok(d), resident(g), tok(RET_WIDTH), tok(RET_WIDTH), tok(RET_WIDTH),
                  tok(RET_WIDTH), st_spec, resident(mask), resident(dec), tok(FFT_WIDTH),
                  resident(w_gates), resident(w_ret), resident(w_fft), resident(w_mix)],
        out_specs=tok(d),
        scratch_shapes=[pltpu.VMEM((tm, RET_WIDTH), BF16), pltpu.VMEM((tm, d), BF16)],
        compiler_params=_params(2),
        name="mixer_out",
    )(ld_f, ld_b, x, g, q, k, v, gs, st, mask, dec, fr, w_gates, w_ret, w_fft, w_mix)


def _rotary_tables(s):
    half = HEAD_DIM // 2
    inv_freq = 1.0 / (ROPE_BASE ** jnp.linspace(0.0, 1.0, half, dtype=F32))
    freq = jnp.concatenate([inv_freq, inv_freq])[None, :]
    sign = jnp.concatenate([-jnp.ones((half,), F32), jnp.ones((half,), F32)])[None, :]
    starts = jnp.arange(0, s, TOKEN_TILE, dtype=F32)[:, None] * freq
    base = jnp.stack([jnp.cos(starts), jnp.sin(starts)], axis=1)[:, :, None, :]
    base = jnp.broadcast_to(base, (base.shape[0], 2, V7X_SUBLANES_F32, HEAD_DIM))
    rows = jnp.arange(TOKEN_TILE, dtype=F32)[:, None] * freq
    cos_r, sin_r = jnp.cos(rows), jnp.sin(rows)
    return base, jnp.stack([cos_r, sin_r, cos_r * sign, sin_r * sign])


def _prepare_weights(g_ffn1, w_ffn1_in, w_ffn1_out, g_mix, w_in, w_ret_out, w_fft_out, w_mix_out,
                     g_ffn2, w_ffn2_in, w_ffn2_out, g_final):
    depth, d, _ = w_in.shape
    n_proj = 4 * RET_WIDTH + FFT_WIDTH
    w_qk = w_in[:, :, :2 * RET_WIDTH].astype(BF16).reshape(depth, d, 2 * RET_HEADS, HEAD_DIM // 2, 2)
    w_qk = jnp.swapaxes(w_qk, 3, 4).reshape(depth, d, 2 * RET_WIDTH)
    w_proj = jnp.concatenate([w_qk, w_in[:, :, 2 * RET_WIDTH:n_proj].astype(BF16)], axis=2)
    gain = lambda g: g[:, None, :]
    return dict(
        g_ffn1=gain(g_ffn1), w_ffn1_in=w_ffn1_in.astype(BF16), w_ffn1_out=w_ffn1_out.astype(BF16),
        g_mix=gain(g_mix), w_proj=w_proj, w_gates=w_in[:, :, n_proj:].astype(BF16),
        w_ret_out=w_ret_out.astype(BF16), w_fft_out=w_fft_out.astype(BF16), w_mix_out=w_mix_out.astype(BF16),
        g_ffn2=gain(g_ffn2), w_ffn2_in=w_ffn2_in.astype(BF16), w_ffn2_out=w_ffn2_out.astype(BF16),
        g_final=g_final[None, None, :],
    )


def _trunk(x, w, ld_f, ld_b, mask, dec):
    b, s, d = x.shape
    assert s % TOKEN_TILE == 0 and TOKEN_TILE % RET_CHUNK == 0 and s % (DFT_N2 * DFT_ROWS) == 0
    n1 = s // DFT_N2
    depth = ld_f.shape[0]
    rot_base, rot_rows = _rotary_tables(s)
    dft_g, dft_m, dft_c = _dft_tables(s)
    x = x.reshape(b * s, d)
    for l in range(depth):
        x = _ffn(x, w["g_ffn1"], w["w_ffn1_in"], w["w_ffn1_out"], w["g_final"], l, final_norm=False)
        q, k, v, gs, f, kv = _mixer_in(x.reshape(b, s, d), w["g_mix"], w["w_proj"], rot_base, rot_rows,
                                       ld_f, ld_b, l)
        st = _ret_states(kv, ld_f, ld_b, l)
        fr = _dft_cols(_dft_rows(f.reshape(b, n1, DFT_N2, FFT_WIDTH), dft_g), dft_m, dft_c)
        x = _mixer_out(x.reshape(b, s, d), w["g_mix"], q, k, v, gs, st, mask, dec, ld_f, ld_b,
                       fr.reshape(b, s, FFT_WIDTH), w["w_gates"], w["w_ret_out"], w["w_fft_out"], w["w_mix_out"], l)
        x = _ffn(x.reshape(b * s, d), w["g_ffn2"], w["w_ffn2_in"], w["w_ffn2_out"], w["g_final"], l,
                 final_norm=(l == depth - 1))
    return x.reshape(b, s, d)


def kernel(x_prompt, x_sample, g_ffn1, w_ffn1_in, w_ffn1_out, g_mix, w_in, ret_log_decay_fwd, ret_log_decay_bwd,
           w_ret_out, w_fft_out, w_mix_out, g_ffn2, w_ffn2_in, w_ffn2_out, g_final):
    w = _prepare_weights(g_ffn1, w_ffn1_in, w_ffn1_out, g_mix, w_in, w_ret_out, w_fft_out, w_mix_out,
                         g_ffn2, w_ffn2_in, w_ffn2_out, g_final)
    mask, dec = _ret_decay(ret_log_decay_fwd, ret_log_decay_bwd, RET_CHUNK)
    return tuple(_trunk(x, w, ret_log_decay_fwd, ret_log_decay_bwd, mask, dec) for x in (x_prompt, x_sample))
```

```python
import functools

import jax
import jax.numpy as jnp
from jax import lax
from jax.experimental import pallas as pl
from jax.experimental.pallas import tpu as pltpu

F32 = jnp.float32
BF16 = jnp.bfloat16

RET_HEADS = 4
HEAD_DIM = 128
RET_WIDTH = RET_HEADS * HEAD_DIM
FFT_GROUPS = 4
FFT_GROUP_DIM = 128
FFT_WIDTH = FFT_GROUPS * FFT_GROUP_DIM
RMS_EPS = 1e-6
GN_EPS = 1e-6
ROPE_BASE = 10000.0

V7X_SUBLANES_F32 = 8
V7X_SUBLANES_BF16 = 16
V7X_MXU_COLUMNS = 256
V7X_VMEM_BYTES = 64 * 1024 * 1024
VMEM_LIMIT_BYTES = V7X_VMEM_BYTES - 8 * 1024 * 1024

TOKEN_TILE = 1024
RET_CHUNK = 256
FF_CHUNK = V7X_MXU_COLUMNS
MIX_CHUNK = V7X_MXU_COLUMNS
DFT_N2 = 128
DFT_ROWS = 2 * V7X_SUBLANES_BF16


def _params(n_grid_dims):
    return pltpu.CompilerParams(
        dimension_semantics=("parallel",) * n_grid_dims,
        vmem_limit_bytes=VMEM_LIMIT_BYTES,
    )


def _resident(shape):
    return pl.BlockSpec(shape, lambda *_: (0,) * len(shape), pipeline_mode=pl.Buffered(1))


def _layer_resident(shape, layer):
    index = (layer,) + (0,) * (len(shape) - 1)
    return pl.BlockSpec((None,) + tuple(shape[1:]), lambda *_: index, pipeline_mode=pl.Buffered(1))


def _smem():
    return pl.BlockSpec(memory_space=pltpu.SMEM)


def _rms_scale(x):
    return lax.rsqrt(jnp.mean(x * x, axis=-1, keepdims=True) + RMS_EPS)


def _rmsnorm(x, g):
    return x * _rms_scale(x) * g


def _dot(a, b):
    return jnp.dot(a, b, preferred_element_type=F32)


def _ffn_kernel(x_ref, g_ref, win_ref, wout_ref, gfin_ref, o_ref, h_ref, *, d_ff, final_norm):
    x = x_ref[...]
    xg = (x * g_ref[...]).astype(BF16)
    scale = _rms_scale(x)
    for lo in range(0, d_ff, FF_CHUNK):
        a = _dot(xg, win_ref[:, lo:lo + FF_CHUNK]) * scale
        b = _dot(xg, win_ref[:, d_ff + lo:d_ff + lo + FF_CHUNK]) * scale
        h_ref[:, lo:lo + FF_CHUNK] = (jax.nn.silu(a) * b).astype(BF16)
    out = x + 0.5 * _dot(h_ref[...], wout_ref[...])
    if final_norm:
        out = _rmsnorm(out, gfin_ref[...])
    o_ref[...] = out


def _ffn(x, g, w_in, w_out, g_final, layer, *, final_norm):
    n_tok, d = x.shape
    d_ff = w_out.shape[1]
    tm = TOKEN_TILE
    assert n_tok % tm == 0 and d_ff % FF_CHUNK == 0
    tile = pl.BlockSpec((tm, d), lambda i: (i, 0))
    return pl.pallas_call(
        functools.partial(_ffn_kernel, d_ff=d_ff, final_norm=final_norm),
        out_shape=jax.ShapeDtypeStruct((n_tok, d), F32),
        grid=(n_tok // tm,),
        in_specs=[tile, _layer_resident(g.shape, layer), _layer_resident(w_in.shape, layer),
                  _layer_resident(w_out.shape, layer), _layer_resident(g_final.shape, 0)],
        out_specs=tile,
        scratch_shapes=[pltpu.VMEM((tm, d_ff), BF16)],
        compiler_params=_params(1),
        name="ffn",
    )(x, g, w_in, w_out, g_final)


def _mixer_in_kernel(ldf_ref, ldb_ref, x_ref, g_ref, w_ref, base_ref, rot_ref,
                     act_ref, f_ref, kv_ref, *, tm, layer):
    x = x_ref[...]
    xg = (x * g_ref[...]).astype(BF16)
    scale = _rms_scale(x)
    cos_b, sin_b = base_ref[0, 0:1, :], base_ref[1, 0:1, :]
    cos = cos_b * rot_ref[0] - sin_b * rot_ref[1]
    sin = sin_b * rot_ref[2] + cos_b * rot_ref[3]
    q = _dot(xg, w_ref[:, 0:RET_WIDTH]) * scale
    k = _dot(xg, w_ref[:, RET_WIDTH:2 * RET_WIDTH]) * scale
    v = (_dot(xg, w_ref[:, 2 * RET_WIDTH:3 * RET_WIDTH]) * scale).astype(BF16)
    act_ref[:, 2 * RET_WIDTH:3 * RET_WIDTH] = v
    act_ref[:, 3 * RET_WIDTH:4 * RET_WIDTH] = (_dot(xg, w_ref[:, 3 * RET_WIDTH:4 * RET_WIDTH]) * scale).astype(BF16)
    f_ref[...] = (_dot(xg, w_ref[:, 4 * RET_WIDTH:4 * RET_WIDTH + FFT_WIDTH]) * scale).astype(BF16)
    row = lax.broadcasted_iota(jnp.int32, (tm, HEAD_DIM), 0).astype(F32)
    k_scale = HEAD_DIM ** -0.5
    for h in range(RET_HEADS):
        cols = slice(h * HEAD_DIM, (h + 1) * HEAD_DIM)
        qh = q[:, cols]
        act_ref[:, cols] = (qh * cos + pltpu.roll(qh, HEAD_DIM // 2, axis=1) * sin).astype(BF16)
        kh = k[:, cols]
        kr = (kh * cos + pltpu.roll(kh, HEAD_DIM // 2, axis=1) * sin) * k_scale
        act_ref[:, RET_WIDTH + h * HEAD_DIM:RET_WIDTH + (h + 1) * HEAD_DIM] = kr.astype(BF16)
        vh = v[:, cols]
        kf = (kr * jnp.exp(ldf_ref[layer, h] * (tm - 1.0 - row))).astype(BF16)
        kb = (kr * jnp.exp(ldb_ref[layer, h] * row)).astype(BF16)
        tn = (((0,), (0,)), ((), ()))
        kv_ref[2 * h] = lax.dot_general(kf, vh, tn, preferred_element_type=F32)
        kv_ref[2 * h + 1] = lax.dot_general(kb, vh, tn, preferred_element_type=F32)


def _mixer_in(x, g, w, rot_base, rot_rows, ld_f, ld_b, layer):
    b, s, d = x.shape
    tm = TOKEN_TILE
    nt = s // tm
    tok = lambda width: pl.BlockSpec((None, tm, width), lambda i, j: (i, j, 0))
    base = pl.BlockSpec((None,) + rot_base.shape[1:], lambda i, j: (j, 0, 0, 0))
    kv_spec = pl.BlockSpec((None, None, 2 * RET_HEADS, HEAD_DIM, HEAD_DIM), lambda i, j: (i, j, 0, 0, 0))
    return pl.pallas_call(
        functools.partial(_mixer_in_kernel, tm=tm, layer=layer),
        out_shape=(jax.ShapeDtypeStruct((b, s, 4 * RET_WIDTH), BF16), jax.ShapeDtypeStruct((b, s, FFT_WIDTH), BF16),
                   jax.ShapeDtypeStruct((b, nt, 2 * RET_HEADS, HEAD_DIM, HEAD_DIM), F32)),
        grid=(b, nt),
        in_specs=[_smem(), _smem(), tok(d), _layer_resident(g.shape, layer), _layer_resident(w.shape, layer),
                  base, _resident(rot_rows.shape)],
        out_specs=(tok(4 * RET_WIDTH), tok(FFT_WIDTH), kv_spec),
        compiler_params=_params(2),
        name="mixer_in",
    )(ld_f, ld_b, x, g, w, rot_base, rot_rows)


def _ret_states_kernel(ldf_ref, ldb_ref, kv_ref, st_ref, *, nt, tm, layer):
    for h in range(RET_HEADS):
        a_f = jnp.exp(jnp.full((HEAD_DIM, HEAD_DIM), ldf_ref[layer, h] * tm, F32))
        a_b = jnp.exp(jnp.full((HEAD_DIM, HEAD_DIM), ldb_ref[layer, h] * tm, F32))
        state = jnp.zeros((HEAD_DIM, HEAD_DIM), F32)
        for t in range(nt):
            st_ref[t, 2 * h] = state
            state = a_f * state + kv_ref[t, 2 * h]
        state = jnp.zeros((HEAD_DIM, HEAD_DIM), F32)
        for t in reversed(range(nt)):
            st_ref[t, 2 * h + 1] = state
            state = a_b * state + kv_ref[t, 2 * h + 1]


def _ret_states(kv, ld_f, ld_b, layer):
    b, nt = kv.shape[:2]
    spec = pl.BlockSpec((None,) + kv.shape[1:], lambda i: (i, 0, 0, 0, 0))
    return pl.pallas_call(
        functools.partial(_ret_states_kernel, nt=nt, tm=TOKEN_TILE, layer=layer),
        out_shape=jax.ShapeDtypeStruct(kv.shape, F32),
        grid=(b,),
        in_specs=[_smem(), _smem(), spec],
        out_specs=spec,
        compiler_params=_params(1),
        name="ret_states",
    )(ld_f, ld_b, kv)


def _ret_decay_kernel(ldf_ref, ldb_ref, mask_ref, dec_ref, *, rc):
    layer = pl.program_id(0)
    i = lax.broadcasted_iota(jnp.int32, (rc, rc), 0)
    j = lax.broadcasted_iota(jnp.int32, (rc, rc), 1)
    dist = (i - j).astype(F32)
    row = lax.broadcasted_iota(jnp.int32, (rc, HEAD_DIM), 0).astype(F32)
    for h in range(RET_HEADS):
        ld_f = ldf_ref[layer, h]
        ld_b = ldb_ref[layer, h]
        mask_ref[h] = jnp.exp(jnp.where(i >= j, ld_f * dist, -ld_b * dist))
        dec_ref[h, 0] = jnp.exp(ld_f * (row + 1.0)).astype(BF16)
        dec_ref[h, 1] = jnp.exp(ld_b * (rc - row)).astype(BF16)
        dec_ref[h, 2] = jnp.exp(ld_f * (rc - 1.0 - row)).astype(BF16)
        dec_ref[h, 3] = jnp.exp(ld_b * row).astype(BF16)


def _ret_decay(ld_f, ld_b, rc):
    depth = ld_f.shape[0]
    return pl.pallas_call(
        functools.partial(_ret_decay_kernel, rc=rc),
        out_shape=(jax.ShapeDtypeStruct((depth, RET_HEADS, rc, rc), F32),
                   jax.ShapeDtypeStruct((depth, RET_HEADS, 4, rc, HEAD_DIM), BF16)),
        grid=(depth,),
        in_specs=[_smem(), _smem()],
        out_specs=(pl.BlockSpec((None, RET_HEADS, rc, rc), lambda l: (l, 0, 0, 0)),
                   pl.BlockSpec((None, RET_HEADS, 4, rc, HEAD_DIM), lambda l: (l, 0, 0, 0, 0))),
        compiler_params=_params(1),
        name="ret_decay",
    )(ld_f, ld_b)


def _retention_tile(ldf_ref, ldb_ref, act_ref, st_ref, mask_ref, dec_ref, o_ref, *, tm, rc, layer):
    nc = tm // rc
    tn = (((0,), (0,)), ((), ()))
    nt_dims = (((1,), (1,)), ((), ()))
    for h in range(RET_HEADS):
        cols = slice(h * HEAD_DIM, (h + 1) * HEAD_DIM)
        q_ref, k_ref, v_ref, gs_ref = (
            act_ref.at[:, n * RET_WIDTH:(n + 1) * RET_WIDTH] for n in range(4))
        q_dec_f, q_dec_b, k_dec_f, k_dec_b = (dec_ref[h, n] for n in range(4))
        rows = [slice(c * rc, (c + 1) * rc) for c in range(nc)]
        fwd = [st_ref[2 * h]]
        bwd = [st_ref[2 * h + 1]]
        if nc > 1:
            a_f = jnp.exp(jnp.full((HEAD_DIM, HEAD_DIM), ldf_ref[layer, h] * rc, F32))
            a_b = jnp.exp(jnp.full((HEAD_DIM, HEAD_DIM), ldb_ref[layer, h] * rc, F32))
            for c in range(nc - 1):
                kv = lax.dot_general(k_ref[rows[c], cols] * k_dec_f, v_ref[rows[c], cols], tn,
                                     preferred_element_type=F32)
                fwd.append(a_f * fwd[-1] + kv)
            for c in range(nc - 1, 0, -1):
                kv = lax.dot_general(k_ref[rows[c], cols] * k_dec_b, v_ref[rows[c], cols], tn,
                                     preferred_element_type=F32)
                bwd.append(a_b * bwd[-1] + kv)
            bwd = bwd[::-1]
        for c in range(nc):
            qc = q_ref[rows[c], cols]
            kc = k_ref[rows[c], cols]
            vc = v_ref[rows[c], cols]
            scores = lax.dot_general(qc, kc, nt_dims, preferred_element_type=F32)
            intra = _dot((scores * mask_ref[h]).astype(BF16), vc)
            q_both = jnp.concatenate([qc * q_dec_f, qc * q_dec_b], axis=1)
            s_both = jnp.concatenate([fwd[c], bwd[c]], axis=0).astype(BF16)
            y = intra + _dot(q_both, s_both)
            yc = y - jnp.mean(y, axis=-1, keepdims=True)
            yn = yc * lax.rsqrt(jnp.mean(yc * yc, axis=-1, keepdims=True) + GN_EPS)
            o_ref[rows[c], cols] = (jax.nn.silu(gs_ref[rows[c], cols].astype(F32)) * yn).astype(BF16)


def _dft_rows_kernel(*refs, n1, t_rows):
    f_refs, g_ref, y_ref, flat_ref = refs[:FFT_GROUPS], refs[FFT_GROUPS], refs[-2], refs[-1]
    sub = V7X_SUBLANES_F32
    for grp, f_ref in enumerate(f_refs):
        f = f_ref[...].astype(F32)
        for part in range(t_rows // sub):
            flat_ref[grp, part] = f[:, part * sub:(part + 1) * sub, :].reshape(n1 * sub, FFT_GROUP_DIM)
    for t in range(t_rows):
        pick = pl.ds(t % sub, n1, stride=sub)
        x = jnp.concatenate([flat_ref[grp, t // sub, pick, :] for grp in range(FFT_GROUPS)], axis=1).astype(BF16)
        y_ref[t] = _dot(g_ref[t], x).astype(BF16)


def _dft_rows(f, g):
    b, n1, n2, width = f.shape
    t_rows = DFT_ROWS
    grp = [pl.BlockSpec((None, n1, t_rows, FFT_GROUP_DIM), functools.partial(lambda i, j, c: (i, 0, j, c), c=c))
           for c in range(FFT_GROUPS)]
    return pl.pallas_call(
        functools.partial(_dft_rows_kernel, n1=n1, t_rows=t_rows),
        out_shape=jax.ShapeDtypeStruct((b, n2, 2 * n1, width), BF16),
        grid=(b, n2 // t_rows),
        in_specs=grp + [pl.BlockSpec((t_rows, 2 * n1, n1), lambda i, j: (j, 0, 0))],
        out_specs=pl.BlockSpec((None, t_rows, 2 * n1, width), lambda i, j: (i, j, 0, 0)),
        scratch_shapes=[pltpu.VMEM((FFT_GROUPS, t_rows // V7X_SUBLANES_F32, n1 * V7X_SUBLANES_F32, FFT_GROUP_DIM),
                                   F32)],
        compiler_params=_params(2),
        name="dft_rows",
    )(*([f] * FFT_GROUPS), g)


def _dft_cols_kernel(y_ref, m_ref, c_ref, o_ref, yflat_ref, oflat_ref, *, n2):
    sub = V7X_SUBLANES_F32
    halves = DFT_ROWS // sub
    group_cols = [slice(grp * FFT_GROUP_DIM, (grp + 1) * FFT_GROUP_DIM) for grp in range(FFT_GROUPS)]
    for part in range(2):
        for grp in range(FFT_GROUPS):
            y = y_ref[:, part, :, group_cols[grp]].astype(F32)
            for half in range(halves):
                yflat_ref[part, grp, half] = y[:, half * sub:(half + 1) * sub, :].reshape(n2 * sub, FFT_GROUP_DIM)
    for t in range(DFT_ROWS):
        half, pick = t // sub, pl.ds(t % sub, n2, stride=sub)
        a = jnp.concatenate(
            [jnp.concatenate([yflat_ref[part, grp, half, pick, :] for grp in range(FFT_GROUPS)], axis=1)
             for part in range(2)], axis=0).astype(BF16)
        y = _dot(m_ref[...], a)
        y_both = jnp.concatenate([jnp.concatenate([y[:n2, cols], y[n2:, cols]], axis=1) for cols in group_cols],
                                 axis=0).astype(BF16)
        fr = _dot(y_both, c_ref[...])
        for grp in range(FFT_GROUPS):
            oflat_ref[grp, half, pick, :] = fr[grp * n2:(grp + 1) * n2]
    o_ref[...] = jnp.concatenate(
        [jnp.concatenate([oflat_ref[grp, half].reshape(n2, sub, FFT_GROUP_DIM) for half in range(halves)], axis=1)
         for grp in range(FFT_GROUPS)], axis=2).astype(BF16)


def _dft_cols(y, m, c):
    b, n2, two_n1, width = y.shape
    n1 = two_n1 // 2
    y = y.reshape(b, n2, 2, n1, width)
    flat_shape = (FFT_GROUPS, DFT_ROWS // V7X_SUBLANES_F32, n2 * V7X_SUBLANES_F32, FFT_GROUP_DIM)
    return pl.pallas_call(
        functools.partial(_dft_cols_kernel, n2=n2),
        out_shape=jax.ShapeDtypeStruct((b, n2, n1, width), BF16),
        grid=(b, n1 // DFT_ROWS),
        in_specs=[pl.BlockSpec((None, n2, 2, DFT_ROWS, width), lambda i, j: (i, 0, 0, j, 0)),
                  _resident(m.shape), _resident(c.shape)],
        out_specs=pl.BlockSpec((None, n2, DFT_ROWS, width), lambda i, j: (i, 0, j, 0)),
        scratch_shapes=[pltpu.VMEM((2,) + flat_shape, F32), pltpu.VMEM(flat_shape, F32)],
        compiler_params=_params(2),
        name="dft_cols",
    )(y, m, c)


def _dft_tables(s):
    n2 = DFT_N2
    n1 = s // n2
    two_pi = 2.0 * jnp.pi
    scale = (FFT_GROUP_DIM * s) ** -0.5
    i1 = jnp.arange(n1, dtype=jnp.int32)
    ang1 = ((i1[:, None] * i1[None, :]) % n1).astype(F32) * (two_pi / n1)
    angt = (i1[None, :] * jnp.arange(n2, dtype=jnp.int32)[:, None]).astype(F32) * (two_pi / s)
    wr, wi = jnp.cos(ang1)[None], -jnp.sin(ang1)[None]
    tr, ti = (jnp.cos(angt) * scale)[:, :, None], (-jnp.sin(angt) * scale)[:, :, None]
    g = jnp.concatenate([wr * tr - wi * ti, wr * ti + wi * tr], axis=1)
    idx = jnp.arange(n2, dtype=jnp.int32)
    ang2 = ((idx[:, None] * idx[None, :]) % n2).astype(F32) * (two_pi / n2)
    c2, s2 = jnp.cos(ang2), jnp.sin(ang2)
    m = jnp.concatenate([jnp.concatenate([c2, s2], axis=1), jnp.concatenate([-s2, c2], axis=1)], axis=0)
    idc = jnp.arange(FFT_GROUP_DIM, dtype=jnp.int32)
    angc = ((idc[:, None] * idc[None, :]) % FFT_GROUP_DIM).astype(F32) * (two_pi / FFT_GROUP_DIM)
    c = jnp.concatenate([jnp.cos(angc), jnp.sin(angc)], axis=0)
    return g.astype(BF16), m.astype(BF16), c.astype(BF16)


def _mixer_out_kernel(ldf_ref, ldb_ref, x_ref, g_ref, act_ref, st_ref, mask_ref, dec_ref,
                      fr_ref, wg_ref, wr_ref, wf_ref, wm_ref, o_ref, ret_ref, merged_ref, *, d, tm, rc, layer):
    _retention_tile(ldf_ref, ldb_ref, act_ref, st_ref, mask_ref, dec_ref, ret_ref, tm=tm, rc=rc, layer=layer)
    x = x_ref[...]
    xg = (x * g_ref[...]).astype(BF16)
    scale = _rms_scale(x)
    for lo in range(0, d, MIX_CHUNK):
        cols = slice(lo, lo + MIX_CHUNK)
        gate_ret = jax.nn.sigmoid(_dot(xg, wg_ref[:, lo:lo + MIX_CHUNK]) * scale)
        gate_fft = jax.nn.sigmoid(_dot(xg, wg_ref[:, d + lo:d + lo + MIX_CHUNK]) * scale)
        merged = gate_ret * _dot(ret_ref[...], wr_ref[:, cols]) + gate_fft * _dot(fr_ref[...], wf_ref[:, cols])
        merged_ref[:, cols] = merged.astype(BF16)
    o_ref[...] = x + _dot(merged_ref[...], wm_ref[...])


def _mixer_out(x, g, act, st, mask, dec, ld_f, ld_b, fr, w_gates, w_ret, w_fft, w_mix, layer):
    b, s, d = x.shape
    tm = TOKEN_TILE
    rc = mask.shape[-1]
    tok = lambda width: pl.BlockSpec((None, tm, width), lambda i, j: (i, j, 0))
    st_spec = pl.BlockSpec((None, None, 2 * RET_HEADS, HEAD_DIM, HEAD_DIM), lambda i, j: (i, j, 0, 0, 0))
    resident = lambda a: _layer_resident(a.shape, layer)
    return pl.pallas_call(
        functools.partial(_mixer_out_kernel, d=d, tm=tm, rc=rc, layer=layer),
        out_shape=jax.ShapeDtypeStruct((b, s, d), F32),
        grid=(b, s // tm),
        in_specs=[_smem(), _smem(), tok(d), resident(g), tok(4 * RET_WIDTH), st_spec, resident(mask), resident(dec),
                  tok(FFT_WIDTH),
                  resident(w_gates), resident(w_ret), resident(w_fft), resident(w_mix)],
        out_specs=tok(d),
        scratch_shapes=[pltpu.VMEM((tm, RET_WIDTH), BF16), pltpu.VMEM((tm, d), BF16)],
        compiler_params=_params(2),
        name="mixer_out",
    )(ld_f, ld_b, x, g, act, st, mask, dec, fr, w_gates, w_ret, w_fft, w_mix)


def _rotary_tables(s):
    half = HEAD_DIM // 2
    inv_freq = 1.0 / (ROPE_BASE ** jnp.linspace(0.0, 1.0, half, dtype=F32))
    freq = jnp.concatenate([inv_freq, inv_freq])[None, :]
    sign = jnp.concatenate([-jnp.ones((half,), F32), jnp.ones((half,), F32)])[None, :]
    starts = jnp.arange(0, s, TOKEN_TILE, dtype=F32)[:, None] * freq
    base = jnp.stack([jnp.cos(starts), jnp.sin(starts)], axis=1)[:, :, None, :]
    base = jnp.broadcast_to(base, (base.shape[0], 2, V7X_SUBLANES_F32, HEAD_DIM))
    rows = jnp.arange(TOKEN_TILE, dtype=F32)[:, None] * freq
    cos_r, sin_r = jnp.cos(rows), jnp.sin(rows)
    return base, jnp.stack([cos_r, sin_r, cos_r * sign, sin_r * sign])


def _prepare_weights(g_ffn1, w_ffn1_in, w_ffn1_out, g_mix, w_in, w_ret_out, w_fft_out, w_mix_out,
                     g_ffn2, w_ffn2_in, w_ffn2_out, g_final):
    depth, d, _ = w_in.shape
    n_proj = 4 * RET_WIDTH + FFT_WIDTH
    w_qk = w_in[:, :, :2 * RET_WIDTH].astype(BF16).reshape(depth, d, 2 * RET_HEADS, HEAD_DIM // 2, 2)
    w_qk = jnp.swapaxes(w_qk, 3, 4).reshape(depth, d, 2 * RET_WIDTH)
    w_proj = jnp.concatenate([w_qk, w_in[:, :, 2 * RET_WIDTH:n_proj].astype(BF16)], axis=2)
    gain = lambda g: g[:, None, :]
    return dict(
        g_ffn1=gain(g_ffn1), w_ffn1_in=w_ffn1_in.astype(BF16), w_ffn1_out=w_ffn1_out.astype(BF16),
        g_mix=gain(g_mix), w_proj=w_proj, w_gates=w_in[:, :, n_proj:].astype(BF16),
        w_ret_out=w_ret_out.astype(BF16), w_fft_out=w_fft_out.astype(BF16), w_mix_out=w_mix_out.astype(BF16),
        g_ffn2=gain(g_ffn2), w_ffn2_in=w_ffn2_in.astype(BF16), w_ffn2_out=w_ffn2_out.astype(BF16),
        g_final=g_final[None, None, :],
    )


def _trunk(x, w, ld_f, ld_b, mask, dec):
    b, s, d = x.shape
    assert s % TOKEN_TILE == 0 and TOKEN_TILE % RET_CHUNK == 0 and s % (DFT_N2 * DFT_ROWS) == 0
    n1 = s // DFT_N2
    depth = ld_f.shape[0]
    rot_base, rot_rows = _rotary_tables(s)
    dft_g, dft_m, dft_c = _dft_tables(s)
    x = x.reshape(b * s, d)
    for l in range(depth):
        x = _ffn(x, w["g_ffn1"], w["w_ffn1_in"], w["w_ffn1_out"], w["g_final"], l, final_norm=False)
        act, f, kv = _mixer_in(x.reshape(b, s, d), w["g_mix"], w["w_proj"], rot_base, rot_rows, ld_f, ld_b, l)
        st = _ret_states(kv, ld_f, ld_b, l)
        fr = _dft_cols(_dft_rows(f.reshape(b, n1, DFT_N2, FFT_WIDTH), dft_g), dft_m, dft_c)
        x = _mixer_out(x.reshape(b, s, d), w["g_mix"], act, st, mask, dec, ld_f, ld_b,
                       fr.reshape(b, s, FFT_WIDTH), w["w_gates"], w["w_ret_out"], w["w_fft_out"], w["w_mix_out"], l)
        x = _ffn(x.reshape(b * s, d), w["g_ffn2"], w["w_ffn2_in"], w["w_ffn2_out"], w["g_final"], l,
                 final_norm=(l == depth - 1))
    return x.reshape(b, s, d)


def kernel(x_prompt, x_sample, g_ffn1, w_ffn1_in, w_ffn1_out, g_mix, w_in, ret_log_decay_fwd, ret_log_decay_bwd,
           w_ret_out, w_fft_out, w_mix_out, g_ffn2, w_ffn2_in, w_ffn2_out, g_final):
    w = _prepare_weights(g_ffn1, w_ffn1_in, w_ffn1_out, g_mix, w_in, w_ret_out, w_fft_out, w_mix_out,
                         g_ffn2, w_ffn2_in, w_ffn2_out, g_final)
    mask, dec = _ret_decay(ret_log_decay_fwd, ret_log_decay_bwd, RET_CHUNK)
    return tuple(_trunk(x, w, ret_log_decay_fwd, ret_log_decay_bwd, mask, dec) for x in (x_prompt, x_sample))
```

```python
import functools

import jax
import jax.numpy as jnp
from jax import lax
from jax.experimental import pallas as pl
from jax.experimental.pallas import tpu as pltpu

F32 = jnp.float32
BF16 = jnp.bfloat16

RET_HEADS = 4
HEAD_DIM = 128
RET_WIDTH = RET_HEADS * HEAD_DIM
FFT_GROUPS = 4
FFT_GROUP_DIM = 128
FFT_WIDTH = FFT_GROUPS * FFT_GROUP_DIM
RMS_EPS = 1e-6
GN_EPS = 1e-6
ROPE_BASE = 10000.0

V7X_SUBLANES_F32 = 8
V7X_SUBLANES_BF16 = 16
V7X_MXU_COLUMNS = 256
V7X_VMEM_BYTES = 64 * 1024 * 1024
VMEM_LIMIT_BYTES = V7X_VMEM_BYTES - 8 * 1024 * 1024

TOKEN_TILE = 1024
RET_CHUNK = 256
FF_CHUNK = V7X_MXU_COLUMNS
MIX_CHUNK = V7X_MXU_COLUMNS
DFT_N2 = 128
DFT_ROWS = 2 * V7X_SUBLANES_BF16


def _params(n_grid_dims):
    return pltpu.CompilerParams(
        dimension_semantics=("parallel",) * n_grid_dims,
        vmem_limit_bytes=VMEM_LIMIT_BYTES,
    )


def _resident(shape):
    return pl.BlockSpec(shape, lambda *_: (0,) * len(shape), pipeline_mode=pl.Buffered(1))


def _layer_resident(shape, layer):
    index = (layer,) + (0,) * (len(shape) - 1)
    return pl.BlockSpec((None,) + tuple(shape[1:]), lambda *_: index, pipeline_mode=pl.Buffered(1))


def _smem():
    return pl.BlockSpec(memory_space=pltpu.SMEM)


def _rms_scale(x):
    return lax.rsqrt(jnp.mean(x * x, axis=-1, keepdims=True) + RMS_EPS)


def _rmsnorm(x, g):
    return x * _rms_scale(x) * g


def _dot(a, b):
    return jnp.dot(a, b, preferred_element_type=F32)


def _ffn_kernel(x_ref, g_ref, win_ref, wout_ref, gfin_ref, o_ref, h_ref, *, d_ff, final_norm):
    x = x_ref[...]
    xg = (x * g_ref[...]).astype(BF16)
    scale = _rms_scale(x)
    for lo in range(0, d_ff, FF_CHUNK):
        a = _dot(xg, win_ref[:, lo:lo + FF_CHUNK]) * scale
        b = _dot(xg, win_ref[:, d_ff + lo:d_ff + lo + FF_CHUNK]) * scale
        h_ref[:, lo:lo + FF_CHUNK] = (jax.nn.silu(a) * b).astype(BF16)
    out = x + 0.5 * _dot(h_ref[...], wout_ref[...])
    if final_norm:
        out = _rmsnorm(out, gfin_ref[...])
    o_ref[...] = out


def _ffn(x, g, w_in, w_out, g_final, layer, *, final_norm):
    n_tok, d = x.shape
    d_ff = w_out.shape[1]
    tm = TOKEN_TILE
    assert n_tok % tm == 0 and d_ff % FF_CHUNK == 0
    tile = pl.BlockSpec((tm, d), lambda i: (i, 0))
    return pl.pallas_call(
        functools.partial(_ffn_kernel, d_ff=d_ff, final_norm=final_norm),
        out_shape=jax.ShapeDtypeStruct((n_tok, d), F32),
        grid=(n_tok // tm,),
        in_specs=[tile, _layer_resident(g.shape, layer), _layer_resident(w_in.shape, layer),
                  _layer_resident(w_out.shape, layer), _layer_resident(g_final.shape, 0)],
        out_specs=tile,
        scratch_shapes=[pltpu.VMEM((tm, d_ff), BF16)],
        compiler_params=_params(1),
        name="ffn",
    )(x, g, w_in, w_out, g_final)


def _mixer_in_kernel(ldf_ref, ldb_ref, x_ref, g_ref, w_ref, base_ref, rot_ref,
                     q_ref, k_ref, v_ref, gs_ref, f_ref, kv_ref, *, tm, layer):
    x = x_ref[...]
    xg = (x * g_ref[...]).astype(BF16)
    scale = _rms_scale(x)
    cos_b, sin_b = base_ref[0, 0:1, :], base_ref[1, 0:1, :]
    cos = cos_b * rot_ref[0] - sin_b * rot_ref[1]
    sin = sin_b * rot_ref[2] + cos_b * rot_ref[3]
    q = _dot(xg, w_ref[:, 0:RET_WIDTH]) * scale
    k = _dot(xg, w_ref[:, RET_WIDTH:2 * RET_WIDTH]) * scale
    v = (_dot(xg, w_ref[:, 2 * RET_WIDTH:3 * RET_WIDTH]) * scale).astype(BF16)
    v_ref[...] = v
    gs_ref[...] = (_dot(xg, w_ref[:, 3 * RET_WIDTH:4 * RET_WIDTH]) * scale).astype(BF16)
    f_ref[...] = (_dot(xg, w_ref[:, 4 * RET_WIDTH:4 * RET_WIDTH + FFT_WIDTH]) * scale).astype(BF16)
    row = lax.broadcasted_iota(jnp.int32, (tm, HEAD_DIM), 0).astype(F32)
    k_scale = HEAD_DIM ** -0.5
    for h in range(RET_HEADS):
        cols = slice(h * HEAD_DIM, (h + 1) * HEAD_DIM)
        qh = q[:, cols]
        q_ref[:, cols] = (qh * cos + pltpu.roll(qh, HEAD_DIM // 2, axis=1) * sin).astype(BF16)
        kh = k[:, cols]
        kr = (kh * cos + pltpu.roll(kh, HEAD_DIM // 2, axis=1) * sin) * k_scale
        k_ref[:, cols] = kr.astype(BF16)
        vh = v[:, cols]
        kf = (kr * jnp.exp(ldf_ref[layer, h] * (tm - 1.0 - row))).astype(BF16)
        kb = (kr * jnp.exp(ldb_ref[layer, h] * row)).astype(BF16)
        tn = (((0,), (0,)), ((), ()))
        kv_ref[2 * h] = lax.dot_general(kf, vh, tn, preferred_element_type=F32).astype(BF16)
        kv_ref[2 * h + 1] = lax.dot_general(kb, vh, tn, preferred_element_type=F32).astype(BF16)


def _mixer_in(x, g, w, rot_base, rot_rows, ld_f, ld_b, layer):
    b, s, d = x.shape
    tm = TOKEN_TILE
    nt = s // tm
    tok = lambda width: pl.BlockSpec((None, tm, width), lambda i, j: (i, j, 0))
    base = pl.BlockSpec((None,) + rot_base.shape[1:], lambda i, j: (j, 0, 0, 0))
    kv_spec = pl.BlockSpec((None, None, 2 * RET_HEADS, HEAD_DIM, HEAD_DIM), lambda i, j: (i, j, 0, 0, 0))
    act = lambda dtype: jax.ShapeDtypeStruct((b, s, RET_WIDTH), dtype)
    return pl.pallas_call(
        functools.partial(_mixer_in_kernel, tm=tm, layer=layer),
        out_shape=(act(BF16), act(BF16), act(BF16), act(BF16), act(BF16),
                   jax.ShapeDtypeStruct((b, nt, 2 * RET_HEADS, HEAD_DIM, HEAD_DIM), BF16)),
        grid=(b, nt),
        in_specs=[_smem(), _smem(), tok(d), _layer_resident(g.shape, layer), _layer_resident(w.shape, layer),
                  base, _resident(rot_rows.shape)],
        out_specs=(tok(RET_WIDTH), tok(RET_WIDTH), tok(RET_WIDTH), tok(RET_WIDTH), tok(FFT_WIDTH), kv_spec),
        compiler_params=_params(2),
        name="mixer_in",
    )(ld_f, ld_b, x, g, w, rot_base, rot_rows)


def _ret_states_kernel(ldf_ref, ldb_ref, kv_ref, st_ref, *, nt, tm, layer):
    for h in range(RET_HEADS):
        a_f = jnp.exp(jnp.full((HEAD_DIM, HEAD_DIM), ldf_ref[layer, h] * tm, F32))
        a_b = jnp.exp(jnp.full((HEAD_DIM, HEAD_DIM), ldb_ref[layer, h] * tm, F32))
        state = jnp.zeros((HEAD_DIM, HEAD_DIM), F32)
        for t in range(nt):
            st_ref[t, 2 * h] = state
            state = a_f * state + kv_ref[t, 2 * h].astype(F32)
        state = jnp.zeros((HEAD_DIM, HEAD_DIM), F32)
        for t in reversed(range(nt)):
            st_ref[t, 2 * h + 1] = state
            state = a_b * state + kv_ref[t, 2 * h + 1].astype(F32)


def _ret_states(kv, ld_f, ld_b, layer):
    b, nt = kv.shape[:2]
    spec = pl.BlockSpec((None,) + kv.shape[1:], lambda i: (i, 0, 0, 0, 0))
    return pl.pallas_call(
        functools.partial(_ret_states_kernel, nt=nt, tm=TOKEN_TILE, layer=layer),
        out_shape=jax.ShapeDtypeStruct(kv.shape, F32),
        grid=(b,),
        in_specs=[_smem(), _smem(), spec],
        out_specs=spec,
        compiler_params=_params(1),
        name="ret_states",
    )(ld_f, ld_b, kv)


def _ret_decay_kernel(ldf_ref, ldb_ref, mask_ref, dec_ref, *, rc):
    layer = pl.program_id(0)
    i = lax.broadcasted_iota(jnp.int32, (rc, rc), 0)
    j = lax.broadcasted_iota(jnp.int32, (rc, rc), 1)
    dist = (i - j).astype(F32)
    row = lax.broadcasted_iota(jnp.int32, (rc, HEAD_DIM), 0).astype(F32)
    for h in range(RET_HEADS):
        ld_f = ldf_ref[layer, h]
        ld_b = ldb_ref[layer, h]
        mask_ref[h] = jnp.exp(jnp.where(i >= j, ld_f * dist, -ld_b * dist))
        dec_ref[h, 0] = jnp.exp(ld_f * (row + 1.0)).astype(BF16)
        dec_ref[h, 1] = jnp.exp(ld_b * (rc - row)).astype(BF16)
        dec_ref[h, 2] = jnp.exp(ld_f * (rc - 1.0 - row)).astype(BF16)
        dec_ref[h, 3] = jnp.exp(ld_b * row).astype(BF16)


def _ret_decay(ld_f, ld_b, rc):
    depth = ld_f.shape[0]
    return pl.pallas_call(
        functools.partial(_ret_decay_kernel, rc=rc),
        out_shape=(jax.ShapeDtypeStruct((depth, RET_HEADS, rc, rc), F32),
                   jax.ShapeDtypeStruct((depth, RET_HEADS, 4, rc, HEAD_DIM), BF16)),
        grid=(depth,),
        in_specs=[_smem(), _smem()],
        out_specs=(pl.BlockSpec((None, RET_HEADS, rc, rc), lambda l: (l, 0, 0, 0)),
                   pl.BlockSpec((None, RET_HEADS, 4, rc, HEAD_DIM), lambda l: (l, 0, 0, 0, 0))),
        compiler_params=_params(1),
        name="ret_decay",
    )(ld_f, ld_b)


def _retention_tile(ldf_ref, ldb_ref, q_ref, k_ref, v_ref, gs_ref, st_ref, mask_ref, dec_ref, o_ref,
                    *, tm, rc, layer):
    nc = tm // rc
    tn = (((0,), (0,)), ((), ()))
    nt_dims = (((1,), (1,)), ((), ()))
    for h in range(RET_HEADS):
        cols = slice(h * HEAD_DIM, (h + 1) * HEAD_DIM)
        q_dec_f, q_dec_b, k_dec_f, k_dec_b = (dec_ref[h, n] for n in range(4))
        rows = [slice(c * rc, (c + 1) * rc) for c in range(nc)]
        fwd = [st_ref[2 * h]]
        bwd = [st_ref[2 * h + 1]]
        if nc > 1:
            a_f = jnp.exp(jnp.full((HEAD_DIM, HEAD_DIM), ldf_ref[layer, h] * rc, F32))
            a_b = jnp.exp(jnp.full((HEAD_DIM, HEAD_DIM), ldb_ref[layer, h] * rc, F32))
            for c in range(nc - 1):
                kv = lax.dot_general(k_ref[rows[c], cols] * k_dec_f, v_ref[rows[c], cols], tn,
                                     preferred_element_type=F32)
                fwd.append(a_f * fwd[-1] + kv)
            for c in range(nc - 1, 0, -1):
                kv = lax.dot_general(k_ref[rows[c], cols] * k_dec_b, v_ref[rows[c], cols], tn,
                                     preferred_element_type=F32)
                bwd.append(a_b * bwd[-1] + kv)
            bwd = bwd[::-1]
        for c in range(nc):
            qc = q_ref[rows[c], cols]
            kc = k_ref[rows[c], cols]
            vc = v_ref[rows[c], cols]
            scores = lax.dot_general(qc, kc, nt_dims, preferred_element_type=F32)
            intra = _dot((scores * mask_ref[h]).astype(BF16), vc)
            q_both = jnp.concatenate([qc * q_dec_f, qc * q_dec_b], axis=1)
            s_both = jnp.concatenate([fwd[c], bwd[c]], axis=0).astype(BF16)
            y = intra + _dot(q_both, s_both)
            yc = y - jnp.mean(y, axis=-1, keepdims=True)
            yn = yc * lax.rsqrt(jnp.mean(yc * yc, axis=-1, keepdims=True) + GN_EPS)
            o_ref[rows[c], cols] = (jax.nn.silu(gs_ref[rows[c], cols].astype(F32)) * yn).astype(BF16)


def _dft_rows_kernel(*refs, n1, t_rows):
    f_refs, g_ref, y_ref, flat_ref = refs[:FFT_GROUPS], refs[FFT_GROUPS], refs[-2], refs[-1]
    sub = V7X_SUBLANES_F32
    for grp, f_ref in enumerate(f_refs):
        f = f_ref[...].astype(F32)
        for part in range(t_rows // sub):
            flat_ref[grp, part] = f[:, part * sub:(part + 1) * sub, :].reshape(n1 * sub, FFT_GROUP_DIM)
    for t in range(t_rows):
        pick = pl.ds(t % sub, n1, stride=sub)
        x = jnp.concatenate([flat_ref[grp, t // sub, pick, :] for grp in range(FFT_GROUPS)], axis=1).astype(BF16)
        y_ref[t] = _dot(g_ref[t], x).astype(BF16)


def _dft_rows(f, g):
    b, n1, n2, width = f.shape
    t_rows = DFT_ROWS
    grp = [pl.BlockSpec((None, n1, t_rows, FFT_GROUP_DIM), functools.partial(lambda i, j, c: (i, 0, j, c), c=c))
           for c in range(FFT_GROUPS)]
    return pl.pallas_call(
        functools.partial(_dft_rows_kernel, n1=n1, t_rows=t_rows),
        out_shape=jax.ShapeDtypeStruct((b, n2, 2 * n1, width), BF16),
        grid=(b, n2 // t_rows),
        in_specs=grp + [pl.BlockSpec((t_rows, 2 * n1, n1), lambda i, j: (j, 0, 0))],
        out_specs=pl.BlockSpec((None, t_rows, 2 * n1, width), lambda i, j: (i, j, 0, 0)),
        scratch_shapes=[pltpu.VMEM((FFT_GROUPS, t_rows // V7X_SUBLANES_F32, n1 * V7X_SUBLANES_F32, FFT_GROUP_DIM),
                                   F32)],
        compiler_params=_params(2),
        name="dft_rows",
    )(*([f] * FFT_GROUPS), g)


def _dft_cols_kernel(y_ref, m_ref, c_ref, o_ref, yflat_ref, oflat_ref, *, n2):
    sub = V7X_SUBLANES_F32
    halves = DFT_ROWS // sub
    group_cols = [slice(grp * FFT_GROUP_DIM, (grp + 1) * FFT_GROUP_DIM) for grp in range(FFT_GROUPS)]
    for part in range(2):
        for grp in range(FFT_GROUPS):
            y = y_ref[:, part, :, group_cols[grp]].astype(F32)
            for half in range(halves):
                yflat_ref[part, grp, half] = y[:, half * sub:(half + 1) * sub, :].reshape(n2 * sub, FFT_GROUP_DIM)
    for t in range(DFT_ROWS):
        half, pick = t // sub, pl.ds(t % sub, n2, stride=sub)
        a = jnp.concatenate(
            [jnp.concatenate([yflat_ref[part, grp, half, pick, :] for grp in range(FFT_GROUPS)], axis=1)
             for part in range(2)], axis=0).astype(BF16)
        y = _dot(m_ref[...], a)
        y_both = jnp.concatenate([jnp.concatenate([y[:n2, cols], y[n2:, cols]], axis=1) for cols in group_cols],
                                 axis=0).astype(BF16)
        fr = _dot(y_both, c_ref[...])
        for grp in range(FFT_GROUPS):
            oflat_ref[grp, half, pick, :] = fr[grp * n2:(grp + 1) * n2]
    o_ref[...] = jnp.concatenate(
        [jnp.concatenate([oflat_ref[grp, half].reshape(n2, sub, FFT_GROUP_DIM) for half in range(halves)], axis=1)
         for grp in range(FFT_GROUPS)], axis=2).astype(BF16)


def _dft_cols(y, m, c):
    b, n2, two_n1, width = y.shape
    n1 = two_n1 // 2
    y = y.reshape(b, n2, 2, n1, width)
    flat_shape = (FFT_GROUPS, DFT_ROWS // V7X_SUBLANES_F32, n2 * V7X_SUBLANES_F32, FFT_GROUP_DIM)
    return pl.pallas_call(
        functools.partial(_dft_cols_kernel, n2=n2),
        out_shape=jax.ShapeDtypeStruct((b, n2, n1, width), BF16),
        grid=(b, n1 // DFT_ROWS),
        in_specs=[pl.BlockSpec((None, n2, 2, DFT_ROWS, width), lambda i, j: (i, 0, 0, j, 0)),
                  _resident(m.shape), _resident(c.shape)],
        out_specs=pl.BlockSpec((None, n2, DFT_ROWS, width), lambda i, j: (i, 0, j, 0)),
        scratch_shapes=[pltpu.VMEM((2,) + flat_shape, F32), pltpu.VMEM(flat_shape, F32)],
        compiler_params=_params(2),
        name="dft_cols",
    )(y, m, c)


def _dft_tables(s):
    n2 = DFT_N2
    n1 = s // n2
    two_pi = 2.0 * jnp.pi
    scale = (FFT_GROUP_DIM * s) ** -0.5
    i1 = jnp.arange(n1, dtype=jnp.int32)
    ang1 = ((i1[:, None] * i1[None, :]) % n1).astype(F32) * (two_pi / n1)
    angt = (i1[None, :] * jnp.arange(n2, dtype=jnp.int32)[:, None]).astype(F32) * (two_pi / s)
    wr, wi = jnp.cos(ang1)[None], -jnp.sin(ang1)[None]
    tr, ti = (jnp.cos(angt) * scale)[:, :, None], (-jnp.sin(angt) * scale)[:, :, None]
    g = jnp.concatenate([wr * tr - wi * ti, wr * ti + wi * tr], axis=1)
    idx = jnp.arange(n2, dtype=jnp.int32)
    ang2 = ((idx[:, None] * idx[None, :]) % n2).astype(F32) * (two_pi / n2)
    c2, s2 = jnp.cos(ang2), jnp.sin(ang2)
    m = jnp.concatenate([jnp.concatenate([c2, s2], axis=1), jnp.concatenate([-s2, c2], axis=1)], axis=0)
    idc = jnp.arange(FFT_GROUP_DIM, dtype=jnp.int32)
    angc = ((idc[:, None] * idc[None, :]) % FFT_GROUP_DIM).astype(F32) * (two_pi / FFT_GROUP_DIM)
    c = jnp.concatenate([jnp.cos(angc), jnp.sin(angc)], axis=0)
    return g.astype(BF16), m.astype(BF16), c.astype(BF16)


def _mixer_out_kernel(ldf_ref, ldb_ref, x_ref, g_ref, q_ref, k_ref, v_ref, gs_ref, st_ref, mask_ref, dec_ref,
                      fr_ref, wg_ref, wr_ref, wf_ref, wm_ref, o_ref, ret_ref, merged_ref, *, d, tm, rc, layer):
    _retention_tile(ldf_ref, ldb_ref, q_ref, k_ref, v_ref, gs_ref, st_ref, mask_ref, dec_ref, ret_ref,
                    tm=tm, rc=rc, layer=layer)
    x = x_ref[...]
    xg = (x * g_ref[...]).astype(BF16)
    scale = _rms_scale(x)
    for lo in range(0, d, MIX_CHUNK):
        cols = slice(lo, lo + MIX_CHUNK)
        gate_ret = jax.nn.sigmoid(_dot(xg, wg_ref[:, lo:lo + MIX_CHUNK]) * scale)
        gate_fft = jax.nn.sigmoid(_dot(xg, wg_ref[:, d + lo:d + lo + MIX_CHUNK]) * scale)
        merged = gate_ret * _dot(ret_ref[...], wr_ref[:, cols]) + gate_fft * _dot(fr_ref[...], wf_ref[:, cols])
        merged_ref[:, cols] = merged.astype(BF16)
    o_ref[...] = x + _dot(merged_ref[...], wm_ref[...])


def _mixer_out(x, g, q, k, v, gs, st, mask, dec, ld_f, ld_b, fr, w_gates, w_ret, w_fft, w_mix, layer):
    b, s, d = x.shape
    tm = TOKEN_TILE
    rc = mask.shape[-1]
    tok = lambda width: pl.BlockSpec((None, tm, width), lambda i, j: (i, j, 0))
    st_spec = pl.BlockSpec((None, None, 2 * RET_HEADS, HEAD_DIM, HEAD_DIM), lambda i, j: (i, j, 0, 0, 0))
    resident = lambda a: _layer_resident(a.shape, layer)
    return pl.pallas_call(
        functools.partial(_mixer_out_kernel, d=d, tm=tm, rc=rc, layer=layer),
        out_shape=jax.ShapeDtypeStruct((b, s, d), F32),
        grid=(b, s // tm),
        in_specs=[_smem(), _smem(), tok(d), resident(g), tok(RET_WIDTH), tok(RET_WIDTH), tok(RET_WIDTH),
                  tok(RET_WIDTH), st_spec, resident(mask), resident(dec), tok(FFT_WIDTH),
                  resident(w_gates), resident(w_ret), resident(w_fft), resident(w_mix)],
        out_specs=tok(d),
        scratch_shapes=[pltpu.VMEM((tm, RET_WIDTH), BF16), pltpu.VMEM((tm, d), BF16)],
        compiler_params=_params(2),
        name="mixer_out",
    )(ld_f, ld_b, x, g, q, k, v, gs, st, mask, dec, fr, w_gates, w_ret, w_fft, w_mix)


def _rotary_tables(s):
    half = HEAD_DIM // 2
    inv_freq = 1.0 / (ROPE_BASE ** jnp.linspace(0.0, 1.0, half, dtype=F32))
    freq = jnp.concatenate([inv_freq, inv_freq])[None, :]
    sign = jnp.concatenate([-jnp.ones((half,), F32), jnp.ones((half,), F32)])[None, :]
    starts = jnp.arange(0, s, TOKEN_TILE, dtype=F32)[:, None] * freq
    base = jnp.stack([jnp.cos(starts), jnp.sin(starts)], axis=1)[:, :, None, :]
    base = jnp.broadcast_to(base, (base.shape[0], 2, V7X_SUBLANES_F32, HEAD_DIM))
    rows = jnp.arange(TOKEN_TILE, dtype=F32)[:, None] * freq
    cos_r, sin_r = jnp.cos(rows), jnp.sin(rows)
    return base, jnp.stack([cos_r, sin_r, cos_r * sign, sin_r * sign])


def _prepare_weights(g_ffn1, w_ffn1_in, w_ffn1_out, g_mix, w_in, w_ret_out, w_fft_out, w_mix_out,
                     g_ffn2, w_ffn2_in, w_ffn2_out, g_final):
    depth, d, _ = w_in.shape
    n_proj = 4 * RET_WIDTH + FFT_WIDTH
    w_qk = w_in[:, :, :2 * RET_WIDTH].astype(BF16).reshape(depth, d, 2 * RET_HEADS, HEAD_DIM // 2, 2)
    w_qk = jnp.swapaxes(w_qk, 3, 4).reshape(depth, d, 2 * RET_WIDTH)
    w_proj = jnp.concatenate([w_qk, w_in[:, :, 2 * RET_WIDTH:n_proj].astype(BF16)], axis=2)
    gain = lambda g: g[:, None, :]
    return dict(
        g_ffn1=gain(g_ffn1), w_ffn1_in=w_ffn1_in.astype(BF16), w_ffn1_out=w_ffn1_out.astype(BF16),
        g_mix=gain(g_mix), w_proj=w_proj, w_gates=w_in[:, :, n_proj:].astype(BF16),
        w_ret_out=w_ret_out.astype(BF16), w_fft_out=w_fft_out.astype(BF16), w_mix_out=w_mix_out.astype(BF16),
        g_ffn2=gain(g_ffn2), w_ffn2_in=w_ffn2_in.astype(BF16), w_ffn2_out=w_ffn2_out.astype(BF16),
        g_final=g_final[None, None, :],
    )


def _trunk(x, w, ld_f, ld_b, mask, dec):
    b, s, d = x.shape
    assert s % TOKEN_TILE == 0 and TOKEN_TILE % RET_CHUNK == 0 and s % (DFT_N2 * DFT_ROWS) == 0
    n1 = s // DFT_N2
    depth = ld_f.shape[0]
    rot_base, rot_rows = _rotary_tables(s)
    dft_g, dft_m, dft_c = _dft_tables(s)
    x = x.reshape(b * s, d)
    for l in range(depth):
        x = _ffn(x, w["g_ffn1"], w["w_ffn1_in"], w["w_ffn1_out"], w["g_final"], l, final_norm=False)
        q, k, v, gs, f, kv = _mixer_in(x.reshape(b, s, d), w["g_mix"], w["w_proj"], rot_base, rot_rows,
                                       ld_f, ld_b, l)
        st = _ret_states(kv, ld_f, ld_b, l)
        fr = _dft_cols(_dft_rows(f.reshape(b, n1, DFT_N2, FFT_WIDTH), dft_g), dft_m, dft_c)
        x = _mixer_out(x.reshape(b, s, d), w["g_mix"], q, k, v, gs, st, mask, dec, ld_f, ld_b,
                       fr.reshape(b, s, FFT_WIDTH), w["w_gates"], w["w_ret_out"], w["w_fft_out"], w["w_mix_out"], l)
        x = _ffn(x.reshape(b * s, d), w["g_ffn2"], w["w_ffn2_in"], w["w_ffn2_out"], w["g_final"], l,
                 final_norm=(l == depth - 1))
    return x.reshape(b, s, d)


def kernel(x_prompt, x_sample, g_ffn1, w_ffn1_in, w_ffn1_out, g_mix, w_in, ret_log_decay_fwd, ret_log_decay_bwd,
           w_ret_out, w_fft_out, w_mix_out, g_ffn2, w_ffn2_in, w_ffn2_out, g_final):
    w = _prepare_weights(g_ffn1, w_ffn1_in, w_ffn1_out, g_mix, w_in, w_ret_out, w_fft_out, w_mix_out,
                         g_ffn2, w_ffn2_in, w_ffn2_out, g_final)
    mask, dec = _ret_decay(ret_log_decay_fwd, ret_log_decay_bwd, RET_CHUNK)
    return tuple(_trunk(x, w, ret_log_decay_fwd, ret_log_decay_bwd, mask, dec) for x in (x_prompt, x_sample))
```
